```python
import jax, jax.numpy as jnp
from jax import lax
import numpy as np

D_MODEL = 1024
BATCH = 8
SEQ = 2048
DEPTH = 4
DEC_BATCH = 128
DEC_SEQ = 8
PAST_LEN = 16384
PAGE_SIZE = 128

N_MIXERS = 3
N_A = (DEPTH + 2) // 3
N_B = (DEPTH + 1) // 3
N_C = DEPTH // 3
CONV_W = 3
CHUNK = 128
B_HEADS = 4
B_HEAD_DIM = D_MODEL // B_HEADS
POOL_WINDOWS = (2, 4, 8, 16)
POOL_GROUPS = len(POOL_WINDOWS)
POOL_GROUP_DIM = D_MODEL // POOL_GROUPS
POOL_STATE = max(POOL_WINDOWS) - 1
D_FF = 2816
EPS = 1e-6

kernel_name = "hybrid_shortconv_chunkmlp_pool_convffn_step"


def rmsnorm(x, g):
    xf = x.astype(jnp.float32)
    y = xf * lax.rsqrt(jnp.mean(xf * xf, axis=-1, keepdims=True) + EPS)
    return (y * g.astype(jnp.float32)).astype(x.dtype)


def causal_dwconv3(x, prev, w):
    L = x.shape[1]
    xp = jnp.concatenate([prev, x], axis=1)
    y = w[0] * xp[:, 0:L]
    for k in range(1, CONV_W):
        y = y + w[k] * xp[:, k:k + L]
    return y, xp[:, -(CONV_W - 1):]


def short_conv_mixer(h, prev, w_in, w_conv, w_out):
    bg, cg, v = jnp.split(h @ w_in, 3, axis=-1)
    y, new_prev = causal_dwconv3(cg * v, prev, w_conv)
    return (bg * y) @ w_out, new_prev


def chunk_mlp_mixer(h, w_in, g_v, w_s, bias, w_out):
    Bn, L, _ = h.shape
    u, v = jnp.split(h @ w_in, 2, axis=-1)
    v = rmsnorm(v, g_v)
    n_chunks = -(-L // CHUNK)
    pad = n_chunks * CHUNK - L
    vp = jnp.pad(v, ((0, 0), (0, pad), (0, 0))).reshape(Bn, n_chunks, CHUNK, B_HEADS, B_HEAD_DIM)
    mask = jnp.tril(jnp.ones((CHUNK, CHUNK), dtype=bool))
    w_m = jnp.where(mask[None], w_s, 0)
    s = jnp.einsum('hts,bnshd->bnthd', w_m, vp) + bias.T[:, :, None]
    s = s.reshape(Bn, n_chunks * CHUNK, D_MODEL)[:, :L]
    return (u * s) @ w_out, v


def pool_mixer(h, prev, start_pos, w_group, scale):
    Bn, L, D = h.shape
    xp = jnp.concatenate([prev, h], axis=1)
    csum = jnp.cumsum(xp.astype(jnp.float32), axis=1)
    csum = jnp.pad(csum, ((0, 0), (1, 0), (0, 0)))
    pos = start_pos + jnp.arange(L)
    means = []
    for gi, w in enumerate(POOL_WINDOWS):
        sl = slice(gi * POOL_GROUP_DIM, (gi + 1) * POOL_GROUP_DIM)
        hi = csum[:, POOL_STATE + 1:POOL_STATE + 1 + L, sl]
        lo = csum[:, POOL_STATE + 1 - w:POOL_STATE + 1 - w + L, sl]
        cnt = jnp.minimum(pos + 1, w).astype(jnp.float32)[None, :, None]
        means.append((hi - lo) / cnt)
    pooled = jnp.concatenate(means, axis=-1).astype(h.dtype)
    d = (pooled - h).reshape(Bn, L, POOL_GROUPS, POOL_GROUP_DIM)
    y = jnp.einsum('blgc,gce->blge', d, w_group).reshape(Bn, L, D)
    return y * scale, xp[:, -POOL_STATE:]


def conv_ffn(h, prev, w_up, w_conv, b_conv, w_down):
    g, a = jnp.split(h @ w_up, 2, axis=-1)
    gc, new_prev = causal_dwconv3(g, prev, w_conv)
    return (jax.nn.silu(gc + b_conv) * a) @ w_down, new_prev


def trunk(x, conv_prev, pool_prev, ffn_prev, start_pos,
          g_mix, g_ffn, g_final,
          a_w_in, a_conv, a_w_out,
          b_w_in, b_g_v, b_w_s, b_bias, b_w_out,
          c_w_group, c_scale,
          f_w_up, f_conv, f_conv_b, f_w_down):
    new_conv, new_pool, new_ffn, new_v = [], [], [], []
    for i in range(DEPTH):
        kind, j = i % N_MIXERS, i // N_MIXERS
        h = rmsnorm(x, g_mix[i])
        if kind == 0:
            y, st = short_conv_mixer(h, conv_prev[j], a_w_in[j], a_conv[j], a_w_out[j])
            new_conv.append(st)
        elif kind == 1:
            y, v = chunk_mlp_mixer(h, b_w_in[j], b_g_v[j], b_w_s[j], b_bias[j], b_w_out[j])
            new_v.append(v)
        else:
            y, st = pool_mixer(h, pool_prev[j], start_pos, c_w_group[j], c_scale[j])
            new_pool.append(st)
        x = x + y
        h = rmsnorm(x, g_ffn[i])
        y, st = conv_ffn(h, ffn_prev[i], f_w_up[i], f_conv[i], f_conv_b[i], f_w_down[i])
        new_ffn.append(st)
        x = x + y
    return (rmsnorm(x, g_final), jnp.stack(new_conv), jnp.stack(new_pool),
            jnp.stack(new_ffn), jnp.stack(new_v))


def setup_inputs(seed: int = 0) -> dict:
    key = jax.random.key(seed)
    ks = jax.random.split(key, 24)
    nrm = lambda k, shape, s: jax.random.normal(k, shape, jnp.float32) * s
    D = D_MODEL
    return {
        "x_prompt": nrm(ks[0], (BATCH, SEQ, D), 1.0),
        "x_sample": nrm(ks[1], (DEC_BATCH, DEC_SEQ, D), 1.0),
        "state_shortconv": nrm(ks[2], (N_A, DEC_BATCH, CONV_W - 1, D), 1.0),
        "state_pool": nrm(ks[3], (N_C, DEC_BATCH, POOL_STATE, D), 1.0),
        "state_ffnconv": nrm(ks[4], (DEPTH, DEC_BATCH, CONV_W - 1, D_FF), 1.0),
        "g_mix": 1.0 + nrm(ks[5], (DEPTH, D), 0.05),
        "g_ffn": 1.0 + nrm(ks[6], (DEPTH, D), 0.05),
        "g_final": 1.0 + nrm(ks[7], (D,), 0.05),
        "a_w_in": nrm(ks[8], (N_A, D, 3 * D), D ** -0.5),
        "a_conv": nrm(ks[9], (N_A, CONV_W, D), CONV_W ** -0.5),
        "a_w_out": nrm(ks[10], (N_A, D, D), D ** -0.5),
        "b_w_in": nrm(ks[11], (N_B, D, 2 * D), D ** -0.5),
        "b_g_v": 1.0 + nrm(ks[12], (N_B, D), 0.05),
        "b_w_s": nrm(ks[13], (N_B, B_HEADS, CHUNK, CHUNK), CHUNK ** -0.5),
        "b_bias": 1.0 + nrm(ks[14], (N_B, B_HEADS, CHUNK), 0.1),
        "b_w_out": nrm(ks[15], (N_B, D, D), D ** -0.5),
        "c_w_group": nrm(ks[16], (N_C, POOL_GROUPS, POOL_GROUP_DIM, POOL_GROUP_DIM), POOL_GROUP_DIM ** -0.5),
        "c_scale": 1.0 + nrm(ks[17], (N_C, D), 0.1),
        "f_w_up": nrm(ks[18], (DEPTH, D, 2 * D_FF), D ** -0.5),
        "f_conv": nrm(ks[19], (DEPTH, CONV_W, D_FF), CONV_W ** -0.5),
        "f_conv_b": nrm(ks[20], (DEPTH, D_FF), 0.02),
        "f_w_down": nrm(ks[21], (DEPTH, D_FF, D), D_FF ** -0.5),
    }


def reference(x_prompt, x_sample, state_shortconv, state_pool, state_ffnconv,
              g_mix, g_ffn, g_final,
              a_w_in, a_conv, a_w_out,
              b_w_in, b_g_v, b_w_s, b_bias, b_w_out,
              c_w_group, c_scale,
              f_w_up, f_conv, f_conv_b, f_w_down):
    params = (g_mix, g_ffn, g_final, a_w_in, a_conv, a_w_out,
              b_w_in, b_g_v, b_w_s, b_bias, b_w_out, c_w_group, c_scale,
              f_w_up, f_conv, f_conv_b, f_w_down)
    dt = x_prompt.dtype
    conv0 = jnp.zeros((N_A, BATCH, CONV_W - 1, D_MODEL), dt)
    pool0 = jnp.zeros((N_C, BATCH, POOL_STATE, D_MODEL), dt)
    ffn0 = jnp.zeros((DEPTH, BATCH, CONV_W - 1, D_FF), dt)
    y_prompt, conv_p, pool_p, ffn_p, _ = trunk(x_prompt, conv0, pool0, ffn0, 0, *params)
    y_sample, conv_s, pool_s, ffn_s, chunkv_s = trunk(
        x_sample, state_shortconv, state_pool, state_ffnconv, PAST_LEN, *params)
    return (y_prompt, y_sample, conv_p, conv_s, pool_p, pool_s, ffn_p, ffn_s, chunkv_s)
```

```python
import functools

import jax
import jax.numpy as jnp
from jax import lax
from jax.experimental import pallas as pl
from jax.experimental.pallas import tpu as pltpu

EPS = 1e-6
CONV_W = 3
CHUNK = 128
POOL_WINDOWS = (2, 4, 8, 16)
POOL_STATE = max(POOL_WINDOWS) - 1

LANE_TILE = 256
SUBLANES = 8
PROMPT_TILE = 512
VMEM_LIMIT_BYTES = 52 * 1024 * 1024

_BF16 = jnp.bfloat16
_F32 = jnp.float32


def _rms(x, g):
    return x * lax.rsqrt(jnp.mean(x * x, axis=-1, keepdims=True) + EPS) * g


def _dot(a, b):
    return jnp.dot(a, b, preferred_element_type=_F32)


def _const_spec(shape):
    zeros = (0,) * len(shape)
    return pl.BlockSpec(shape, lambda *_: zeros, pipeline_mode=pl.Buffered(1))


def _params(n_grid):
    return pltpu.CompilerParams(
        dimension_semantics=("arbitrary",) * n_grid,
        vmem_limit_bytes=VMEM_LIMIT_BYTES)


def _prompt_roll_history(buf, hist, rows):
    j = pl.program_id(1)

    @pl.when(j == 0)
    def _():
        buf[0:hist, :] = jnp.zeros((hist, buf.shape[1]), buf.dtype)

    @pl.when(j > 0)
    def _():
        buf[0:hist, :] = buf[rows:rows + hist, :]


def _conv3(buf, cw_ref, cur, hist, step, rows, cols):
    lo = hist - 2 * step
    mid = hist - step
    return (cw_ref[0:1, cols] * buf[lo:lo + rows, cols]
            + cw_ref[1:2, cols] * buf[mid:mid + rows, cols]
            + cw_ref[2:3, cols] * cur)


def _ffn_kernel(*refs, rows, hist, step, prompt, final_norm):
    it = iter(refs)
    x_ref = next(it)
    state_ref = None if prompt else next(it)
    gn_ref, wup_ref, cw_ref, cb_ref, wdn_ref = (next(it) for _ in range(5))
    gfin_ref = next(it) if final_norm else None
    o_ref, st_ref = next(it), next(it)
    h_s, act_s, gbuf = next(it), next(it), next(it)

    d_ff = wdn_ref.shape[0]
    if prompt:
        _prompt_roll_history(gbuf, hist, rows)
    else:
        gbuf[0:hist, :] = state_ref[...]

    h_s[...] = _rms(x_ref[...], gn_ref[...]).astype(_BF16)
    for c in range(d_ff // LANE_TILE):
        cols = slice(c * LANE_TILE, (c + 1) * LANE_TILE)
        acols = slice(d_ff + c * LANE_TILE, d_ff + (c + 1) * LANE_TILE)
        g = _dot(h_s[...], wup_ref[:, cols])
        a = _dot(h_s[...], wup_ref[:, acols])
        gbuf[hist:hist + rows, cols] = g
        y = _conv3(gbuf, cw_ref, g, hist, step, rows, cols) + cb_ref[:, cols]
        act_s[:, cols] = (y * jax.nn.sigmoid(y) * a).astype(_BF16)

    out = x_ref[...] + _dot(act_s[...], wdn_ref[...])
    if final_norm:
        out = _rms(out, gfin_ref[...])
    o_ref[...] = out

    tail = slice(hist + rows - 2 * step, hist + rows)
    if prompt:
        @pl.when(pl.program_id(1) == pl.num_programs(1) - 1)
        def _():
            st_ref[...] = gbuf[tail, :]
    else:
        st_ref[...] = gbuf[tail, :]


def _ffn(x, state, gn, wup, cw, cb, wdn, gfin, *, prompt, batch):
    n, d = x.shape
    d_ff = wdn.shape[0]
    final_norm = gfin is not None
    if prompt:
        rows, hist, step = PROMPT_TILE, SUBLANES, 1
        n_t = n // batch // rows
        grid = (batch, n_t)
        x_spec = pl.BlockSpec((rows, d), lambda b, j: (b * n_t + j, 0))
        st_spec = pl.BlockSpec((None, CONV_W - 1, d_ff), lambda b, j: (b, 0, 0))
        st_shape = jax.ShapeDtypeStruct((batch, CONV_W - 1, d_ff), _F32)
        ins, in_specs = [x], [x_spec]
    else:
        rows, hist, step = n, (CONV_W - 1) * batch, batch
        grid = (1,)
        x_spec = pl.BlockSpec((rows, d), lambda i: (0, 0))
        st_spec = pl.BlockSpec((hist, d_ff), lambda i: (0, 0))
        st_shape = jax.ShapeDtypeStruct((hist, d_ff), _F32)
        ins, in_specs = [x, state], [x_spec, st_spec]
    weights = [gn, wup, cw, cb, wdn] + ([gfin] if final_norm else [])
    ins += weights
    in_specs += [_const_spec(w.shape) for w in weights]
    return pl.pallas_call(
        functools.partial(_ffn_kernel, rows=rows, hist=hist, step=step,
                          prompt=prompt, final_norm=final_norm),
        grid=grid,
        in_specs=in_specs,
        out_specs=[x_spec, st_spec],
        out_shape=[jax.ShapeDtypeStruct((n, d), _F32), st_shape],
        scratch_shapes=[pltpu.VMEM((rows, d), _BF16),
                        pltpu.VMEM((rows, d_ff), _BF16),
                        pltpu.VMEM((hist + rows, d_ff), _F32)],
        compiler_params=_params(len(grid)),
        name="ffn_prompt" if prompt else "ffn_sample",
    )(*ins)


def _sconv_kernel(*refs, rows, hist, step, prompt):
    it = iter(refs)
    x_ref = next(it)
    state_ref = None if prompt else next(it)
    gn_ref, win_ref, cw_ref, wout_ref = (next(it) for _ in range(4))
    o_ref, st_ref = next(it), next(it)
    h_s, z_s, ubuf = next(it), next(it), next(it)

    d = x_ref.shape[1]
    if prompt:
        _prompt_roll_history(ubuf, hist, rows)
    else:
        ubuf[0:hist, :] = state_ref[...]

    h_s[...] = _rms(x_ref[...], gn_ref[...]).astype(_BF16)
    for c in range(d // LANE_TILE):
        cols = slice(c * LANE_TILE, (c + 1) * LANE_TILE)
        bg = _dot(h_s[...], win_ref[:, cols])
        cg = _dot(h_s[...], win_ref[:, d + c * LANE_TILE:d + (c + 1) * LANE_TILE])
        v = _dot(h_s[...], win_ref[:, 2 * d + c * LANE_TILE:2 * d + (c + 1) * LANE_TILE])
        u = cg * v
        ubuf[hist:hist + rows, cols] = u
        y = _conv3(ubuf, cw_ref, u, hist, step, rows, cols)
        z_s[:, cols] = (bg * y).astype(_BF16)

    o_ref[...] = x_ref[...] + _dot(z_s[...], wout_ref[...])

    tail = slice(hist + rows - 2 * step, hist + rows)
    if prompt:
        @pl.when(pl.program_id(1) == pl.num_programs(1) - 1)
        def _():
            st_ref[...] = ubuf[tail, :]
    else:
        st_ref[...] = ubuf[tail, :]


def _sconv(x, state, gn, win, cw, wout, *, prompt, batch):
    n, d = x.shape
    if prompt:
        rows, hist, step = PROMPT_TILE, SUBLANES, 1
        n_t = n // batch // rows
        grid = (batch, n_t)
        x_spec = pl.BlockSpec((rows, d), lambda b, j: (b * n_t + j, 0))
        st_spec = pl.BlockSpec((None, CONV_W - 1, d), lambda b, j: (b, 0, 0))
        st_shape = jax.ShapeDtypeStruct((batch, CONV_W - 1, d), _F32)
        ins, in_specs = [x], [x_spec]
    else:
        rows, hist, step = n, (CONV_W - 1) * batch, batch
        grid = (1,)
        x_spec = pl.BlockSpec((rows, d), lambda i: (0, 0))
        st_spec = pl.BlockSpec((hist, d), lambda i: (0, 0))
        st_shape = jax.ShapeDtypeStruct((hist, d), _F32)
        ins, in_specs = [x, state], [x_spec, st_spec]
    weights = [gn, win, cw, wout]
    ins += weights
    in_specs += [_const_spec(w.shape) for w in weights]
    return pl.pallas_call(
        functools.partial(_sconv_kernel, rows=rows, hist=hist, step=step, prompt=prompt),
        grid=grid,
        in_specs=in_specs,
        out_specs=[x_spec, st_spec],
        out_shape=[jax.ShapeDtypeStruct((n, d), _F32), st_shape],
        scratch_shapes=[pltpu.VMEM((rows, d), _BF16),
                        pltpu.VMEM((rows, d), _BF16),
                        pltpu.VMEM((hist + rows, d), _F32)],
        compiler_params=_params(len(grid)),
        name="sconv_prompt" if prompt else "sconv_sample",
    )(*ins)


def _cmlp_prompt_kernel(x_ref, gn_ref, win_ref, gv_ref, ws_ref, bias_ref, wout_ref,
                        o_ref, h_s, vb_s, z_s, *, rows):
    d = x_ref.shape[1]
    n_heads = ws_ref.shape[0]
    head_dim = d // n_heads
    h_s[...] = _rms(x_ref[...], gn_ref[...]).astype(_BF16)
    v = _dot(h_s[...], win_ref[:, d:2 * d])
    vb_s[...] = _rms(v, gv_ref[...]).astype(_BF16)

    t_idx = lax.broadcasted_iota(jnp.int32, (CHUNK, CHUNK), 0)
    s_idx = lax.broadcasted_iota(jnp.int32, (CHUNK, CHUNK), 1)
    causal = t_idx >= s_idx
    for hd in range(n_heads):
        cols = slice(hd * head_dim, (hd + 1) * head_dim)
        u = _dot(h_s[...], win_ref[:, cols])
        w_m = jnp.where(causal, ws_ref[hd], 0.0).astype(_BF16)
        bias = bias_ref[:, hd:hd + 1]
        for n in range(rows // CHUNK):
            rws = slice(n * CHUNK, (n + 1) * CHUNK)
            s = _dot(w_m, vb_s[rws, cols]) + bias
            z_s[rws, cols] = (u[rws, :] * s).astype(_BF16)
    o_ref[...] = x_ref[...] + _dot(z_s[...], wout_ref[...])


def _cmlp_prompt(x, gn, win, gv, ws, bias_t, wout, *, batch):
    n, d = x.shape
    rows = PROMPT_TILE
    x_spec = pl.BlockSpec((rows, d), lambda i: (i, 0))
    weights = [gn, win, gv, ws, bias_t, wout]
    return pl.pallas_call(
        functools.partial(_cmlp_prompt_kernel, rows=rows),
        grid=(n // rows,),
        in_specs=[x_spec] + [_const_spec(w.shape) for w in weights],
        out_specs=x_spec,
        out_shape=jax.ShapeDtypeStruct((n, d), _F32),
        scratch_shapes=[pltpu.VMEM((rows, d), _BF16),
                        pltpu.VMEM((rows, d), _BF16),
                        pltpu.VMEM((rows, d), _BF16)],
        compiler_params=_params(1),
        name="cmlp_prompt",
    )(x, *weights)


def _cmlp_sample_kernel(wsm_ref, bsm_ref, x_ref, gn_ref, win_ref, gv_ref, wout_ref,
                        o_ref, v_ref, h_s, z_s, *, batch, n_steps, n_heads):
    d = x_ref.shape[1]
    head_dim = d // n_heads
    h_s[...] = _rms(x_ref[...], gn_ref[...]).astype(_BF16)
    v_ref[...] = _rms(_dot(h_s[...], win_ref[:, d:2 * d]), gv_ref[...])
    for hd in range(n_heads):
        cols = slice(hd * head_dim, (hd + 1) * head_dim)
        u = _dot(h_s[...], win_ref[:, cols])
        for t in range(n_steps):
            rws = slice(t * batch, (t + 1) * batch)
            s = jnp.full((batch, head_dim), bsm_ref[hd * n_steps + t], _F32)
            for k in range(t + 1):
                w = wsm_ref[(hd * n_steps + t) * n_steps + k]
                s = s + w * v_ref[k * batch:(k + 1) * batch, cols]
            z_s[rws, cols] = (u[rws, :] * s).astype(_BF16)
    o_ref[...] = x_ref[...] + _dot(z_s[...], wout_ref[...])


def _cmlp_sample(x, gn, win, gv, ws_small, bias_small, wout, *, batch):
    n, d = x.shape
    n_steps = n // batch
    n_heads = ws_small.shape[0]
    x_spec = pl.BlockSpec((n, d), lambda i: (0, 0))
    smem = pl.BlockSpec(memory_space=pltpu.SMEM)
    weights = [gn, win, gv, wout]
    return pl.pallas_call(
        functools.partial(_cmlp_sample_kernel, batch=batch, n_steps=n_steps, n_heads=n_heads),
        grid=(1,),
        in_specs=[smem, smem, x_spec] + [_const_spec(w.shape) for w in weights],
        out_specs=[x_spec, x_spec],
        out_shape=[jax.ShapeDtypeStruct((n, d), _F32)] * 2,
        scratch_shapes=[pltpu.VMEM((n, d), _BF16), pltpu.VMEM((n, d), _BF16)],
        compiler_params=_params(1),
        name="cmlp_sample",
    )(ws_small.reshape(-1), bias_small.reshape(-1), x, *weights)


def _pool_kernel(*refs, rows, hist, step, prompt):
    it = iter(refs)
    x_ref = next(it)
    state_ref = None if prompt else next(it)
    gn_ref, wg_ref, sc_ref = (next(it) for _ in range(3))
    o_ref, st_ref = next(it), next(it)
    hbuf = next(it)

    d = x_ref.shape[1]
    gdim = wg_ref.shape[1]
    if prompt:
        _prompt_roll_history(hbuf, hist, rows)
        pos = pl.program_id(1) * rows + lax.broadcasted_iota(jnp.int32, (rows, 1), 0)
    else:
        hbuf[0:hist, :] = state_ref[...]
    hbuf[hist:hist + rows, :] = _rms(x_ref[...], gn_ref[...])

    for gi, w in enumerate(POOL_WINDOWS):
        cols = slice(gi * gdim, (gi + 1) * gdim)
        h = hbuf[hist:hist + rows, cols]
        tot = h
        for k in range(1, w):
            tot = tot + hbuf[hist - k * step:hist - k * step + rows, cols]
        if prompt:
            cnt = jnp.minimum(pos + 1, w).astype(_F32)
        else:
            cnt = float(w)
        dlt = (tot / cnt - h).astype(_BF16)
        o_ref[:, cols] = x_ref[:, cols] + _dot(dlt, wg_ref[gi]) * sc_ref[:, cols]

    tail = slice(hist + rows - POOL_STATE * step, hist + rows)
    if prompt:
        @pl.when(pl.program_id(1) == pl.num_programs(1) - 1)
        def _():
            st_ref[...] = hbuf[tail, :]
    else:
        st_ref[...] = hbuf[tail, :]


def _pool(x, state, gn, wg, sc, *, prompt, batch):
    n, d = x.shape
    if prompt:
        rows, hist, step = PROMPT_TILE, 2 * SUBLANES, 1
        n_t = n // batch // rows
        grid = (batch, n_t)
        x_spec = pl.BlockSpec((rows, d), lambda b, j: (b * n_t + j, 0))
        st_spec = pl.BlockSpec((None, POOL_STATE, d), lambda b, j: (b, 0, 0))
        st_shape = jax.ShapeDtypeStruct((batch, POOL_STATE, d), _F32)
        ins, in_specs = [x], [x_spec]
    else:
        rows, hist, step = n, POOL_STATE * batch, batch
        grid = (1,)
        x_spec = pl.BlockSpec((rows, d), lambda i: (0, 0))
        st_spec = pl.BlockSpec((hist, d), lambda i: (0, 0))
        st_shape = jax.ShapeDtypeStruct((hist, d), _F32)
        ins, in_specs = [x, state], [x_spec, st_spec]
    weights = [gn, wg, sc]
    ins += weights
    in_specs += [_const_spec(w.shape) for w in weights]
    return pl.pallas_call(
        functools.partial(_pool_kernel, rows=rows, hist=hist, step=step, prompt=prompt),
        grid=grid,
        in_specs=in_specs,
        out_specs=[x_spec, st_spec],
        out_shape=[jax.ShapeDtypeStruct((n, d), _F32), st_shape],
        scratch_shapes=[pltpu.VMEM((hist + rows, d), _F32)],
        compiler_params=_params(len(grid)),
        name="pool_prompt" if prompt else "pool_sample",
    )(*ins)


def _seq_major(state):
    b, k, c = state.shape
    return jnp.swapaxes(state, 0, 1).reshape(k * b, c)


def _batch_major(state, batch):
    kb, c = state.shape
    return jnp.swapaxes(state.reshape(kb // batch, batch, c), 0, 1)


def _trunk(x, conv_prev, pool_prev, ffn_prev, w, *, prompt, batch):
    depth = w["g_mix"].shape[0]
    new_conv, new_pool, new_ffn, new_v = [], [], [], []
    for i in range(depth):
        kind, j = i % 3, i // 3
        gn = w["g_mix"][i][None, :]
        if kind == 0:
            st = None if prompt else conv_prev[j]
            x, st = _sconv(x, st, gn, w["a_w_in"][j], w["a_conv"][j], w["a_w_out"][j],
                           prompt=prompt, batch=batch)
            new_conv.append(st)
        elif kind == 1:
            gv = w["b_g_v"][j][None, :]
            if prompt:
                x = _cmlp_prompt(x, gn, w["b_w_in"][j], gv, w["b_w_s"][j],
                                 w["b_bias"][j].T, w["b_w_out"][j], batch=batch)
            else:
                n_steps = x.shape[0] // batch
                x, v = _cmlp_sample(x, gn, w["b_w_in"][j], gv,
                                    w["b_w_s"][j][:, :n_steps, :n_steps],
                                    w["b_bias"][j][:, :n_steps], w["b_w_out"][j], batch=batch)
                new_v.append(v)
        else:
            st = None if prompt else pool_prev[j]
            x, st = _pool(x, st, gn, w["c_w_group"][j], w["c_scale"][j][None, :],
                          prompt=prompt, batch=batch)
            new_pool.append(st)
        gfin = w["g_final"][None, :] if i == depth - 1 else None
        st = None if prompt else ffn_prev[i]
        x, st = _ffn(x, st, w["g_ffn"][i][None, :], w["f_w_up"][i], w["f_conv"][i],
                     w["f_conv_b"][i][None, :], w["f_w_down"][i], gfin,
                     prompt=prompt, batch=batch)
        new_ffn.append(st)
    return x, new_conv, new_pool, new_ffn, new_v


def kernel(x_prompt, x_sample, state_shortconv, state_pool, state_ffnconv, g_mix, g_ffn, g_final, a_w_in, a_conv, a_w_out, b_w_in, b_g_v, b_w_s, b_bias, b_w_out, c_w_group, c_scale, f_w_up, f_conv, f_conv_b, f_w_down):
    batch, seq, d = x_prompt.shape
    dec_batch, dec_seq, _ = x_sample.shape
    w = dict(g_mix=g_mix, g_ffn=g_ffn, g_final=g_final,
             a_w_in=a_w_in.astype(_BF16), a_conv=a_conv, a_w_out=a_w_out.astype(_BF16),
             b_w_in=b_w_in.astype(_BF16), b_g_v=b_g_v, b_w_s=b_w_s, b_bias=b_bias,
             b_w_out=b_w_out.astype(_BF16),
             c_w_group=c_w_group.astype(_BF16), c_scale=c_scale,
             f_w_up=f_w_up.astype(_BF16), f_conv=f_conv, f_conv_b=f_conv_b,
             f_w_down=f_w_down.astype(_BF16))

    yp, conv_p, pool_p, ffn_p, _ = _trunk(
        x_prompt.reshape(batch * seq, d), None, None, None, w, prompt=True, batch=batch)

    sm = jax.vmap(_seq_major)
    ys, conv_s, pool_s, ffn_s, v_s = _trunk(
        jnp.swapaxes(x_sample, 0, 1).reshape(dec_seq * dec_batch, d),
        sm(state_shortconv), sm(state_pool), sm(state_ffnconv), w,
        prompt=False, batch=dec_batch)
    bm = functools.partial(_batch_major, batch=dec_batch)

    return (yp.reshape(batch, seq, d),
            bm(ys),
            jnp.stack(conv_p), jnp.stack([bm(s) for s in conv_s]),
            jnp.stack(pool_p), jnp.stack([bm(s) for s in pool_s]),
            jnp.stack(ffn_p), jnp.stack([bm(s) for s in ffn_s]),
            jnp.stack([bm(v) for v in v_s]))
```

```python
import functools

import jax
import jax.numpy as jnp
from jax import lax
from jax.experimental import pallas as pl
from jax.experimental.pallas import tpu as pltpu

EPS = 1e-6
CONV_W = 3
CHUNK = 128
POOL_WINDOWS = (2, 4, 8, 16)
POOL_STATE = max(POOL_WINDOWS) - 1

LANE_TILE = 256
SUBLANES = 8
TILE_ROWS = 512
VMEM_LIMIT_BYTES = 56 * 1024 * 1024

_BF16 = jnp.bfloat16
_F32 = jnp.float32


def _rms(x, g):
    return x * lax.rsqrt(jnp.mean(x * x, axis=-1, keepdims=True) + EPS) * g


def _dot(a, b):
    return jnp.dot(a, b, preferred_element_type=_F32)


def _layer_spec(arr, layer):
    blk = (None,) + arr.shape[1:]
    idx = (layer,) + (0,) * (arr.ndim - 1)
    return pl.BlockSpec(blk, lambda *_: idx, pipeline_mode=pl.Buffered(1))


def _conv3_chunk(buf, hist_ref, cw_ref, cur, cols, *, hist, step, rows):
    buf[0:hist, :] = hist_ref[:, cols]
    buf[hist:hist + rows, :] = cur
    lo = hist - 2 * step
    mid = hist - step
    return (cw_ref[0:1, cols] * buf[lo:lo + rows, :]
            + cw_ref[1:2, cols] * buf[mid:mid + rows, :]
            + cw_ref[2:3, cols] * cur)


def _save_tail(buf, hist_ref, st_ref, cols, *, hist, step, rows, prompt):
    st_ref[:, cols] = buf[hist + rows - 2 * step:hist + rows, :]
    if prompt:
        hist_ref[:, cols] = buf[rows:rows + hist, :]


def _sconv_part(x_ref, o_ref, hist_ref, st_ref, gn_ref, win_ref, cw_ref, wout_ref,
                h_s, z_s, cbuf, *, hist, step, rows, prompt):
    d = x_ref.shape[1]
    h_s[...] = _rms(x_ref[...], gn_ref[...]).astype(_BF16)
    for c in range(d // LANE_TILE):
        cols = slice(c * LANE_TILE, (c + 1) * LANE_TILE)
        buf = cbuf.at[c % 2]
        bg = _dot(h_s[...], win_ref[:, cols])
        cg = _dot(h_s[...], win_ref[:, d + c * LANE_TILE:d + (c + 1) * LANE_TILE])
        v = _dot(h_s[...], win_ref[:, 2 * d + c * LANE_TILE:2 * d + (c + 1) * LANE_TILE])
        u = cg * v
        y = _conv3_chunk(buf, hist_ref, cw_ref, u, cols, hist=hist, step=step, rows=rows)
        z_s[:, cols] = (bg * y).astype(_BF16)
        _save_tail(buf, hist_ref, st_ref, cols, hist=hist, step=step, rows=rows, prompt=prompt)
    o_ref[...] = x_ref[...] + _dot(z_s[...], wout_ref[...])


def _cmlp_prompt_part(x_ref, o_ref, gn_ref, win_ref, gv_ref, ws_ref, bias_ref, wout_ref,
                      h_s, z_s, vb_s, *, rows):
    d = x_ref.shape[1]
    n_heads = ws_ref.shape[0]
    head_dim = d // n_heads
    h_s[...] = _rms(x_ref[...], gn_ref[...]).astype(_BF16)
    vb_s[...] = _rms(_dot(h_s[...], win_ref[:, d:2 * d]), gv_ref[...]).astype(_BF16)

    t_idx = lax.broadcasted_iota(jnp.int32, (CHUNK, CHUNK), 0)
    s_idx = lax.broadcasted_iota(jnp.int32, (CHUNK, CHUNK), 1)
    causal = t_idx >= s_idx
    for hd in range(n_heads):
        cols = slice(hd * head_dim, (hd + 1) * head_dim)
        u = _dot(h_s[...], win_ref[:, cols])
        w_m = jnp.where(causal, ws_ref[hd], 0.0).astype(_BF16)
        bias = bias_ref[:, hd:hd + 1]
        for n in range(rows // CHUNK):
            rws = slice(n * CHUNK, (n + 1) * CHUNK)
            s = _dot(w_m, vb_s[rws, cols]) + bias
            z_s[rws, cols] = (u[rws, :] * s).astype(_BF16)
    o_ref[...] = x_ref[...] + _dot(z_s[...], wout_ref[...])


def _cmlp_sample_part(x_ref, o_ref, v_ref, wsm_ref, bsm_ref, gn_ref, win_ref, gv_ref, wout_ref,
                      h_s, z_s, *, step, rows):
    d = x_ref.shape[1]
    n_steps = rows // step
    n_heads = bsm_ref.shape[0] // n_steps
    head_dim = d // n_heads
    h_s[...] = _rms(x_ref[...], gn_ref[...]).astype(_BF16)
    v_ref[...] = _rms(_dot(h_s[...], win_ref[:, d:2 * d]), gv_ref[...])
    for hd in range(n_heads):
        cols = slice(hd * head_dim, (hd + 1) * head_dim)
        u = _dot(h_s[...], win_ref[:, cols])
        for t in range(n_steps):
            rws = slice(t * step, (t + 1) * step)
            s = jnp.full((step, head_dim), bsm_ref[hd * n_steps + t], _F32)
            for k in range(t + 1):
                w = wsm_ref[(hd * n_steps + t) * n_steps + k]
                s = s + w * v_ref[k * step:(k + 1) * step, cols]
            z_s[rws, cols] = (u[rws, :] * s).astype(_BF16)
    o_ref[...] = x_ref[...] + _dot(z_s[...], wout_ref[...])


def _pool_prompt_part(x_ref, o_ref, st_ref, gn_ref, wg_ref, sc_ref, hbuf, *, rows):
    hist = 2 * SUBLANES
    gdim = wg_ref.shape[1]
    j = pl.program_id(1)

    @pl.when(j == 0)
    def _():
        hbuf[0:hist, :] = jnp.zeros((hist, hbuf.shape[1]), _F32)

    @pl.when(j > 0)
    def _():
        hbuf[0:hist, :] = hbuf[rows:rows + hist, :]

    hbuf[hist:hist + rows, :] = _rms(x_ref[...], gn_ref[...])
    pos = j * rows + lax.broadcasted_iota(jnp.int32, (rows, 1), 0)
    for gi, w in enumerate(POOL_WINDOWS):
        cols = slice(gi * gdim, (gi + 1) * gdim)
        h = hbuf[hist:hist + rows, cols]
        tot = h
        for k in range(1, w):
            tot = tot + hbuf[hist - k:hist - k + rows, cols]
        cnt = jnp.minimum(pos + 1, w).astype(_F32)
        dlt = (tot / cnt - h).astype(_BF16)
        o_ref[:, cols] = x_ref[:, cols] + _dot(dlt, wg_ref[gi]) * sc_ref[:, cols]
    st_ref[...] = hbuf[hist + rows - POOL_STATE:hist + rows, :]


def _pool_sample_part(x_ref, o_ref, state_ref, st_ref, gn_ref, wg_ref, sc_ref, hbuf, *, step, rows):
    hist = POOL_STATE * step
    gdim = wg_ref.shape[1]
    hbuf[...] = _rms(x_ref[...], gn_ref[...])
    for gi, w in enumerate(POOL_WINDOWS):
        cols = slice(gi * gdim, (gi + 1) * gdim)
        h = hbuf[:, cols]
        tot = h
        for k in range(1, w):
            back = k * step
            if back >= rows:
                shifted = state_ref[hist - back:hist - back + rows, cols]
            else:
                shifted = jnp.concatenate(
                    [state_ref[hist - back:hist, cols], hbuf[0:rows - back, cols]], axis=0)
            tot = tot + shifted
        dlt = (tot / float(w) - h).astype(_BF16)
        o_ref[:, cols] = x_ref[:, cols] + _dot(dlt, wg_ref[gi]) * sc_ref[:, cols]
    st_ref[0:hist - rows, :] = state_ref[rows:hist, :]
    st_ref[hist - rows:hist, :] = hbuf[...]


def _ffn_part(o_ref, hist_ref, st_ref, gn_ref, wup_ref, cw_ref, cb_ref, wdn_ref, gfin_ref,
              h_s, act_s, cbuf, *, hist, step, rows, prompt):
    d_ff = wdn_ref.shape[0]
    h_s[...] = _rms(o_ref[...], gn_ref[...]).astype(_BF16)
    for c in range(d_ff // LANE_TILE):
        cols = slice(c * LANE_TILE, (c + 1) * LANE_TILE)
        buf = cbuf.at[c % 2]
        g = _dot(h_s[...], wup_ref[:, cols])
        a = _dot(h_s[...], wup_ref[:, d_ff + c * LANE_TILE:d_ff + (c + 1) * LANE_TILE])
        y = _conv3_chunk(buf, hist_ref, cw_ref, g, cols, hist=hist, step=step, rows=rows)
        y = y + cb_ref[:, cols]
        act_s[:, cols] = (y * jax.nn.sigmoid(y) * a).astype(_BF16)
        _save_tail(buf, hist_ref, st_ref, cols, hist=hist, step=step, rows=rows, prompt=prompt)
    out = o_ref[...] + _dot(act_s[...], wdn_ref[...])
    if gfin_ref is not None:
        out = _rms(out, gfin_ref[...])
    o_ref[...] = out


def _layer_kernel(*refs, kind, prompt, final_norm, hist, step, rows):
    it = iter(refs)
    nxt = lambda n=1: next(it) if n == 1 else [next(it) for _ in range(n)]
    conv_kw = dict(hist=hist, step=step, rows=rows, prompt=prompt)
    if kind == 1 and not prompt:
        wsm_ref, bsm_ref = nxt(2)
    x_ref = nxt()
    mix_state = nxt() if (not prompt and kind != 1) else None
    ffn_state = nxt() if not prompt else None
    n_mix_w = {0: 4, 1: 6 if prompt else 4, 2: 3}[kind]
    mix_w = nxt(n_mix_w)
    ffn_w = nxt(5)
    gfin_ref = nxt() if final_norm else None
    o_ref = nxt()
    mix_out = nxt() if not (prompt and kind == 1) else None
    ffn_st = nxt()
    h_s, act_s, fbuf = nxt(3)
    if prompt:
        ffn_carry = nxt()

        @pl.when(pl.program_id(1) == 0)
        def _():
            ffn_carry[...] = jnp.zeros(ffn_carry.shape, _F32)
    else:
        ffn_carry = ffn_state

    if kind == 0:
        z_s, mbuf = nxt(2)
        if prompt:
            mix_carry = nxt()

            @pl.when(pl.program_id(1) == 0)
            def _():
                mix_carry[...] = jnp.zeros(mix_carry.shape, _F32)
        else:
            mix_carry = mix_state
        _sconv_part(x_ref, o_ref, mix_carry, mix_out, *mix_w, h_s, z_s, mbuf, **conv_kw)
    elif kind == 1:
        if prompt:
            z_s, vb_s = nxt(2)
            _cmlp_prompt_part(x_ref, o_ref, *mix_w, h_s, z_s, vb_s, rows=rows)
        else:
            z_s = nxt()
            _cmlp_sample_part(x_ref, o_ref, mix_out, wsm_ref, bsm_ref, *mix_w, h_s, z_s,
                              step=step, rows=rows)
    else:
        hbuf = nxt()
        if prompt:
            _pool_prompt_part(x_ref, o_ref, mix_out, *mix_w, hbuf, rows=rows)
        else:
            _pool_sample_part(x_ref, o_ref, mix_state, mix_out, *mix_w, hbuf, step=step, rows=rows)

    _ffn_part(o_ref, ffn_carry, ffn_st, *ffn_w, gfin_ref, h_s, act_s, fbuf, **conv_kw)


def _layer(x, i, mix_state, ffn_state, w, *, prompt, group):
    n, d = x.shape
    d_ff = w["f_w_down"].shape[1]
    depth = w["g_mix"].shape[0]
    kind, j = i % 3, i // 3
    final_norm = i == depth - 1
    rows = TILE_ROWS
    if prompt:
        n_t = n // group // rows
        grid = (group, n_t)
        hist, step = SUBLANES, 1
        x_spec = pl.BlockSpec((rows, d), lambda b, t: (b * n_t + t, 0))

        def state_spec(n_rows, width):
            return pl.BlockSpec((None, n_rows, width), lambda b, t: (b, 0, 0))

        def state_shape(n_rows, width):
            return jax.ShapeDtypeStruct((group, n_rows, width), _F32)
    else:
        grid = (n // rows,)
        step = group
        hist = (CONV_W - 1) * step
        x_spec = pl.BlockSpec((rows, d), lambda g: (g, 0))

        def state_spec(n_rows, width, layer=None, **kw):
            if layer is None:
                return pl.BlockSpec((n_rows * step, width), lambda g: (g, 0))
            return pl.BlockSpec((None, n_rows * step, width), lambda g: (layer, g, 0), **kw)

        def state_shape(n_rows, width):
            return jax.ShapeDtypeStruct((grid[0] * n_rows * step, width), _F32)

    ins, in_specs, outs, out_specs, scratch = [], [], [], [], []

    def add_in(arr, spec):
        ins.append(arr)
        in_specs.append(spec)

    def add_w(name, layer):
        add_in(w[name], _layer_spec(w[name], layer))

    if kind == 1 and not prompt:
        n_steps = rows // step
        smem = pl.BlockSpec(memory_space=pltpu.SMEM)
        add_in(w["b_w_s"][j, :, :n_steps, :n_steps].reshape(-1), smem)
        add_in(w["b_bias"][j, :, :n_steps].reshape(-1), smem)
    add_in(x, x_spec)
    if not prompt:
        if kind == 0:
            add_in(mix_state, state_spec(CONV_W - 1, d, j))
        elif kind == 2:
            add_in(mix_state, state_spec(POOL_STATE, d, j, pipeline_mode=pl.Buffered(1)))
        add_in(ffn_state, state_spec(CONV_W - 1, d_ff, i))
    add_w("g_mix", i)
    if kind == 0:
        for name in ("a_w_in", "a_conv", "a_w_out"):
            add_w(name, j)
    elif kind == 1:
        names = ("b_w_in", "b_g_v", "b_w_s", "b_bias_t", "b_w_out") if prompt else (
            "b_w_in", "b_g_v", "b_w_out")
        for name in names:
            add_w(name, j)
    else:
        for name in ("c_w_group", "c_scale"):
            add_w(name, j)
    for name in ("g_ffn", "f_w_up", "f_conv", "f_conv_b", "f_w_down"):
        add_w(name, i)
    if final_norm:
        add_in(w["g_final"], pl.BlockSpec(w["g_final"].shape, lambda *_: (0, 0)))

    outs.append(jax.ShapeDtypeStruct((n, d), _F32))
    out_specs.append(x_spec)
    if kind == 0:
        outs.append(state_shape(CONV_W - 1, d))
        out_specs.append(state_spec(CONV_W - 1, d))
    elif kind == 1 and not prompt:
        outs.append(jax.ShapeDtypeStruct((n, d), _F32))
        out_specs.append(x_spec)
    elif kind == 2:
        outs.append(state_shape(POOL_STATE, d))
        out_specs.append(state_spec(POOL_STATE, d))
    outs.append(state_shape(CONV_W - 1, d_ff))
    out_specs.append(state_spec(CONV_W - 1, d_ff))

    scratch += [pltpu.VMEM((rows, d), _BF16), pltpu.VMEM((rows, d_ff), _BF16),
                pltpu.VMEM((2, hist + rows, LANE_TILE), _F32)]
    if prompt:
        scratch.append(pltpu.VMEM((hist, d_ff), _F32))
    if kind == 0:
        scratch += [pltpu.VMEM((rows, d), _BF16), pltpu.VMEM((2, hist + rows, LANE_TILE), _F32)]
        if prompt:
            scratch.append(pltpu.VMEM((hist, d), _F32))
    elif kind == 1:
        scratch.append(pltpu.VMEM((rows, d), _BF16))
        if prompt:
            scratch.append(pltpu.VMEM((rows, d), _BF16))
    else:
        scratch.append(pltpu.VMEM((2 * SUBLANES + rows if prompt else rows, d), _F32))

    res = pl.pallas_call(
        functools.partial(_layer_kernel, kind=kind, prompt=prompt, final_norm=final_norm,
                          hist=hist, step=step, rows=rows),
        grid=grid,
        in_specs=in_specs,
        out_specs=out_specs,
        out_shape=outs,
        scratch_shapes=scratch,
        compiler_params=pltpu.CompilerParams(
            dimension_semantics=("arbitrary",) * len(grid),
            vmem_limit_bytes=VMEM_LIMIT_BYTES),
        name=f"layer{i}_{'prompt' if prompt else 'sample'}",
    )(*ins)
    x_new, ffn_new = res[0], res[-1]
    mix_new = res[1] if len(res) == 3 else None
    return x_new, mix_new, ffn_new


def _trunk(x, conv_prev, pool_prev, ffn_prev, w, *, prompt, group):
    depth = w["g_mix"].shape[0]
    new = {0: [], 1: [], 2: []}
    new_ffn = []
    for i in range(depth):
        kind, j = i % 3, i // 3
        mix_state = conv_prev if kind == 0 else pool_prev
        x, mix_new, ffn_new = _layer(x, i, mix_state, ffn_prev, w, prompt=prompt, group=group)
        if mix_new is not None:
            new[kind].append(mix_new)
        new_ffn.append(ffn_new)
    return x, jnp.stack(new[0]), jnp.stack(new[2]), jnp.stack(new_ffn), new[1]


def _to_groups(a, group):
    *lead, b, k, c = a.shape
    a = a.reshape(*lead, b // group, group, k, c)
    a = jnp.swapaxes(a, -3, -2)
    return a.reshape(*lead, b * k, c)


def _from_groups(a, group, k):
    *lead, n, c = a.shape
    a = a.reshape(*lead, n // (k * group), k, group, c)
    a = jnp.swapaxes(a, -3, -2)
    return a.reshape(*lead, n // k, k, c)


def kernel(x_prompt, x_sample, state_shortconv, state_pool, state_ffnconv, g_mix, g_ffn, g_final, a_w_in, a_conv, a_w_out, b_w_in, b_g_v, b_w_s, b_bias, b_w_out, c_w_group, c_scale, f_w_up, f_conv, f_conv_b, f_w_down):
    batch, seq, d = x_prompt.shape
    dec_batch, dec_seq, _ = x_sample.shape
    row = lambda a: a[:, None, :]
    w = dict(g_mix=row(g_mix), g_ffn=row(g_ffn), g_final=g_final[None, :],
             a_w_in=a_w_in.astype(_BF16), a_conv=a_conv, a_w_out=a_w_out.astype(_BF16),
             b_w_in=b_w_in.astype(_BF16), b_g_v=row(b_g_v), b_w_s=b_w_s, b_bias=b_bias,
             b_bias_t=jnp.swapaxes(b_bias, 1, 2), b_w_out=b_w_out.astype(_BF16),
             c_w_group=c_w_group.astype(_BF16), c_scale=row(c_scale),
             f_w_up=f_w_up.astype(_BF16), f_conv=f_conv, f_conv_b=row(f_conv_b),
             f_w_down=f_w_down.astype(_BF16))

    yp, conv_p, pool_p, ffn_p, _ = _trunk(
        x_prompt.reshape(batch * seq, d), None, None, None, w, prompt=True, group=batch)

    group = TILE_ROWS // dec_seq
    tg = functools.partial(_to_groups, group=group)
    ys, conv_s, pool_s, ffn_s, v_s = _trunk(
        tg(x_sample), tg(state_shortconv), tg(state_pool), tg(state_ffnconv), w,
        prompt=False, group=group)
    fg = functools.partial(_from_groups, group=group)

    return (yp.reshape(batch, seq, d),
            fg(ys, k=dec_seq),
            conv_p, fg(conv_s, k=CONV_W - 1),
            pool_p, fg(pool_s, k=POOL_STATE),
            ffn_p, fg(ffn_s, k=CONV_W - 1),
            fg(jnp.stack(v_s), k=dec_seq))
```

```python
import functools

import jax
import jax.numpy as jnp
from jax import lax
from jax.experimental import pallas as pl
from jax.experimental.pallas import tpu as pltpu

EPS = 1e-6
CONV_W = 3
CHUNK = 128
POOL_WINDOWS = (2, 4, 8, 16)
POOL_STATE = max(POOL_WINDOWS) - 1

LANE_TILE = 256
SUBLANES = 8
TILE_ROWS = 512
VMEM_LIMIT_BYTES = 56 * 1024 * 1024

_BF16 = jnp.bfloat16
_F32 = jnp.float32


def _rms(x, g):
    return x * lax.rsqrt(jnp.mean(x * x, axis=-1, keepdims=True) + EPS) * g


def _dot(a, b):
    return jnp.dot(a, b, preferred_element_type=_F32)


def _layer_spec(arr, layer):
    blk = (None,) + arr.shape[1:]
    idx = (layer,) + (0,) * (arr.ndim - 1)
    return pl.BlockSpec(blk, lambda *_: idx, pipeline_mode=pl.Buffered(1))


def _delayed(cur, hist_ref, cols, k, *, step, prompt):
    rows = cur.shape[0]
    if prompt:
        rolled = pltpu.roll(cur, k, 0)
        row = lax.broadcasted_iota(jnp.int32, (SUBLANES, cur.shape[1]), 0)
        top = jnp.where(row < k, pltpu.roll(hist_ref[:, cols], k, 0), rolled[0:SUBLANES, :])
        return jnp.concatenate([top, rolled[SUBLANES:, :]], axis=0)
    n_hist = hist_ref.shape[0]
    return jnp.concatenate([hist_ref[n_hist - k * step:n_hist, cols], cur[0:rows - k * step, :]],
                           axis=0)


def _conv3(cur, hist_ref, cw_ref, cols, *, step, prompt):
    return (cw_ref[0:1, cols] * _delayed(cur, hist_ref, cols, 2, step=step, prompt=prompt)
            + cw_ref[1:2, cols] * _delayed(cur, hist_ref, cols, 1, step=step, prompt=prompt)
            + cw_ref[2:3, cols] * cur)


def _save_tail(cur, hist_ref, st_ref, cols, *, step, prompt):
    rows = cur.shape[0]
    st_ref[:, cols] = cur[rows - (CONV_W - 1) * step:rows, :]
    if prompt:
        hist_ref[:, cols] = cur[rows - SUBLANES:rows, :]


def _sconv_part(x_ref, o_ref, hist_ref, st_ref, gn_ref, win_ref, cw_ref, wout_ref,
                h_s, z_s, *, step, prompt):
    d = x_ref.shape[1]
    h_s[...] = _rms(x_ref[...], gn_ref[...]).astype(_BF16)
    for c in range(d // LANE_TILE):
        cols = slice(c * LANE_TILE, (c + 1) * LANE_TILE)
        bg = _dot(h_s[...], win_ref[:, cols])
        cg = _dot(h_s[...], win_ref[:, d + c * LANE_TILE:d + (c + 1) * LANE_TILE])
        v = _dot(h_s[...], win_ref[:, 2 * d + c * LANE_TILE:2 * d + (c + 1) * LANE_TILE])
        u = cg * v
        y = _conv3(u, hist_ref, cw_ref, cols, step=step, prompt=prompt)
        z_s[:, cols] = (bg * y).astype(_BF16)
        _save_tail(u, hist_ref, st_ref, cols, step=step, prompt=prompt)
    o_ref[...] = x_ref[...] + _dot(z_s[...], wout_ref[...])


def _cmlp_prompt_part(x_ref, o_ref, gn_ref, win_ref, gv_ref, ws_ref, bias_ref, wout_ref,
                      h_s, z_s, vb_s, *, rows):
    d = x_ref.shape[1]
    n_heads = ws_ref.shape[0]
    head_dim = d // n_heads
    h_s[...] = _rms(x_ref[...], gn_ref[...]).astype(_BF16)
    vb_s[...] = _rms(_dot(h_s[...], win_ref[:, d:2 * d]), gv_ref[...]).astype(_BF16)

    t_idx = lax.broadcasted_iota(jnp.int32, (CHUNK, CHUNK), 0)
    s_idx = lax.broadcasted_iota(jnp.int32, (CHUNK, CHUNK), 1)
    causal = t_idx >= s_idx
    for hd in range(n_heads):
        cols = slice(hd * head_dim, (hd + 1) * head_dim)
        u = _dot(h_s[...], win_ref[:, cols])
        w_m = jnp.where(causal, ws_ref[hd], 0.0).astype(_BF16)
        bias = bias_ref[:, hd:hd + 1]
        for n in range(rows // CHUNK):
            rws = slice(n * CHUNK, (n + 1) * CHUNK)
            s = _dot(w_m, vb_s[rws, cols]) + bias
            z_s[rws, cols] = (u[rws, :] * s).astype(_BF16)
    o_ref[...] = x_ref[...] + _dot(z_s[...], wout_ref[...])


def _cmlp_sample_part(x_ref, o_ref, v_ref, wsm_ref, bsm_ref, gn_ref, win_ref, gv_ref, wout_ref,
                      h_s, z_s, *, step, rows):
    d = x_ref.shape[1]
    n_steps = rows // step
    n_heads = bsm_ref.shape[0] // n_steps
    head_dim = d // n_heads
    h_s[...] = _rms(x_ref[...], gn_ref[...]).astype(_BF16)
    v_ref[...] = _rms(_dot(h_s[...], win_ref[:, d:2 * d]), gv_ref[...])
    for hd in range(n_heads):
        cols = slice(hd * head_dim, (hd + 1) * head_dim)
        u = _dot(h_s[...], win_ref[:, cols])
        for t in range(n_steps):
            rws = slice(t * step, (t + 1) * step)
            s = jnp.full((step, head_dim), bsm_ref[hd * n_steps + t], _F32)
            for k in range(t + 1):
                w = wsm_ref[(hd * n_steps + t) * n_steps + k]
                s = s + w * v_ref[k * step:(k + 1) * step, cols]
            z_s[rws, cols] = (u[rws, :] * s).astype(_BF16)
    o_ref[...] = x_ref[...] + _dot(z_s[...], wout_ref[...])


def _pool_prompt_part(x_ref, o_ref, st_ref, gn_ref, wg_ref, sc_ref, hbuf, *, rows):
    hist = 2 * SUBLANES
    gdim = wg_ref.shape[1]
    j = pl.program_id(1)

    @pl.when(j == 0)
    def _():
        hbuf[0:hist, :] = jnp.zeros((hist, hbuf.shape[1]), _F32)

    @pl.when(j > 0)
    def _():
        hbuf[0:hist, :] = hbuf[rows:rows + hist, :]

    hbuf[hist:hist + rows, :] = _rms(x_ref[...], gn_ref[...])
    pos = j * rows + lax.broadcasted_iota(jnp.int32, (rows, 1), 0)
    for gi, w in enumerate(POOL_WINDOWS):
        cols = slice(gi * gdim, (gi + 1) * gdim)
        h = hbuf[hist:hist + rows, cols]
        tot = h
        for k in range(1, w):
            tot = tot + hbuf[hist - k:hist - k + rows, cols]
        cnt = jnp.minimum(pos + 1, w).astype(_F32)
        dlt = (tot / cnt - h).astype(_BF16)
        o_ref[:, cols] = x_ref[:, cols] + _dot(dlt, wg_ref[gi]) * sc_ref[:, cols]
    st_ref[...] = hbuf[hist + rows - POOL_STATE:hist + rows, :]


def _pool_sample_part(x_ref, o_ref, state_ref, st_ref, gn_ref, wg_ref, sc_ref, hbuf, *, step, rows):
    hist = POOL_STATE * step
    gdim = wg_ref.shape[1]
    hbuf[...] = _rms(x_ref[...], gn_ref[...])
    for gi, w in enumerate(POOL_WINDOWS):
        cols = slice(gi * gdim, (gi + 1) * gdim)
        h = hbuf[:, cols]
        tot = h
        for k in range(1, w):
            back = k * step
            if back >= rows:
                shifted = state_ref[hist - back:hist - back + rows, cols]
            else:
                shifted = jnp.concatenate(
                    [state_ref[hist - back:hist, cols], hbuf[0:rows - back, cols]], axis=0)
            tot = tot + shifted
        dlt = (tot / float(w) - h).astype(_BF16)
        o_ref[:, cols] = x_ref[:, cols] + _dot(dlt, wg_ref[gi]) * sc_ref[:, cols]
    st_ref[0:hist - rows, :] = state_ref[rows:hist, :]
    st_ref[hist - rows:hist, :] = hbuf[...]


def _ffn_part(o_ref, hist_ref, st_ref, gn_ref, wup_ref, cw_ref, cb_ref, wdn_ref, gfin_ref,
              h_s, act_s, *, step, prompt):
    d_ff = wdn_ref.shape[0]
    h_s[...] = _rms(o_ref[...], gn_ref[...]).astype(_BF16)
    for c in range(d_ff // LANE_TILE):
        cols = slice(c * LANE_TILE, (c + 1) * LANE_TILE)
        g = _dot(h_s[...], wup_ref[:, cols])
        a = _dot(h_s[...], wup_ref[:, d_ff + c * LANE_TILE:d_ff + (c + 1) * LANE_TILE])
        y = _conv3(g, hist_ref, cw_ref, cols, step=step, prompt=prompt) + cb_ref[:, cols]
        act_s[:, cols] = (y * jax.nn.sigmoid(y) * a).astype(_BF16)
        _save_tail(g, hist_ref, st_ref, cols, step=step, prompt=prompt)
    out = o_ref[...] + _dot(act_s[...], wdn_ref[...])
    if gfin_ref is not None:
        out = _rms(out, gfin_ref[...])
    o_ref[...] = out


def _layer_kernel(*refs, kind, prompt, final_norm, step, rows):
    it = iter(refs)
    nxt = lambda n=1: next(it) if n == 1 else [next(it) for _ in range(n)]
    conv_kw = dict(step=step, prompt=prompt)
    if kind == 1 and not prompt:
        wsm_ref, bsm_ref = nxt(2)
    x_ref = nxt()
    mix_state = nxt() if (not prompt and kind != 1) else None
    ffn_state = nxt() if not prompt else None
    n_mix_w = {0: 4, 1: 6 if prompt else 4, 2: 3}[kind]
    mix_w = nxt(n_mix_w)
    ffn_w = nxt(5)
    gfin_ref = nxt() if final_norm else None
    o_ref = nxt()
    mix_out = nxt() if not (prompt and kind == 1) else None
    ffn_st = nxt()
    h_s, act_s = nxt(2)
    if prompt:
        ffn_carry = nxt()

        @pl.when(pl.program_id(1) == 0)
        def _():
            ffn_carry[...] = jnp.zeros(ffn_carry.shape, _F32)
    else:
        ffn_carry = ffn_state

    if kind == 0:
        z_s = nxt()
        if prompt:
            mix_carry = nxt()

            @pl.when(pl.program_id(1) == 0)
            def _():
                mix_carry[...] = jnp.zeros(mix_carry.shape, _F32)
        else:
            mix_carry = mix_state
        _sconv_part(x_ref, o_ref, mix_carry, mix_out, *mix_w, h_s, z_s, **conv_kw)
    elif kind == 1:
        if prompt:
            z_s, vb_s = nxt(2)
            _cmlp_prompt_part(x_ref, o_ref, *mix_w, h_s, z_s, vb_s, rows=rows)
        else:
            z_s = nxt()
            _cmlp_sample_part(x_ref, o_ref, mix_out, wsm_ref, bsm_ref, *mix_w, h_s, z_s,
                              step=step, rows=rows)
    else:
        hbuf = nxt()
        if prompt:
            _pool_prompt_part(x_ref, o_ref, mix_out, *mix_w, hbuf, rows=rows)
        else:
            _pool_sample_part(x_ref, o_ref, mix_state, mix_out, *mix_w, hbuf, step=step, rows=rows)

    _ffn_part(o_ref, ffn_carry, ffn_st, *ffn_w, gfin_ref, h_s, act_s, **conv_kw)


def _layer(x, i, mix_state, ffn_state, w, *, prompt, group):
    n, d = x.shape
    d_ff = w["f_w_down"].shape[1]
    depth = w["g_mix"].shape[0]
    kind, j = i % 3, i // 3
    final_norm = i == depth - 1
    rows = TILE_ROWS
    if prompt:
        n_t = n // group // rows
        grid = (group, n_t)
        step = 1
        x_spec = pl.BlockSpec((rows, d), lambda b, t: (b * n_t + t, 0))

        def state_spec(n_rows, width):
            return pl.BlockSpec((None, n_rows, width), lambda b, t: (b, 0, 0))

        def state_shape(n_rows, width):
            return jax.ShapeDtypeStruct((group, n_rows, width), _F32)
    else:
        grid = (n // rows,)
        step = group
        x_spec = pl.BlockSpec((rows, d), lambda g: (g, 0))

        def state_spec(n_rows, width, layer=None, **kw):
            if layer is None:
                return pl.BlockSpec((n_rows * step, width), lambda g: (g, 0))
            return pl.BlockSpec((None, n_rows * step, width), lambda g: (layer, g, 0), **kw)

        def state_shape(n_rows, width):
            return jax.ShapeDtypeStruct((grid[0] * n_rows * step, width), _F32)

    ins, in_specs, outs, out_specs, scratch = [], [], [], [], []

    def add_in(arr, spec):
        ins.append(arr)
        in_specs.append(spec)

    def add_w(name, layer):
        add_in(w[name], _layer_spec(w[name], layer))

    if kind == 1 and not prompt:
        n_steps = rows // step
        smem = pl.BlockSpec(memory_space=pltpu.SMEM)
        add_in(w["b_w_s"][j, :, :n_steps, :n_steps].reshape(-1), smem)
        add_in(w["b_bias"][j, :, :n_steps].reshape(-1), smem)
    add_in(x, x_spec)
    if not prompt:
        if kind == 0:
            add_in(mix_state, state_spec(CONV_W - 1, d, j))
        elif kind == 2:
            add_in(mix_state, state_spec(POOL_STATE, d, j, pipeline_mode=pl.Buffered(1)))
        add_in(ffn_state, state_spec(CONV_W - 1, d_ff, i))
    add_w("g_mix", i)
    if kind == 0:
        for name in ("a_w_in", "a_conv", "a_w_out"):
            add_w(name, j)
    elif kind == 1:
        names = ("b_w_in", "b_g_v", "b_w_s", "b_bias_t", "b_w_out") if prompt else (
            "b_w_in", "b_g_v", "b_w_out")
        for name in names:
            add_w(name, j)
    else:
        for name in ("c_w_group", "c_scale"):
            add_w(name, j)
    for name in ("g_ffn", "f_w_up", "f_conv", "f_conv_b", "f_w_down"):
        add_w(name, i)
    if final_norm:
        add_in(w["g_final"], pl.BlockSpec(w["g_final"].shape, lambda *_: (0, 0)))

    outs.append(jax.ShapeDtypeStruct((n, d), _F32))
    out_specs.append(x_spec)
    if kind == 0:
        outs.append(state_shape(CONV_W - 1, d))
        out_specs.append(state_spec(CONV_W - 1, d))
    elif kind == 1 and not prompt:
        outs.append(jax.ShapeDtypeStruct((n, d), _F32))
        out_specs.append(x_spec)
    elif kind == 2:
        outs.append(state_shape(POOL_STATE, d))
        out_specs.append(state_spec(POOL_STATE, d))
    outs.append(state_shape(CONV_W - 1, d_ff))
    out_specs.append(state_spec(CONV_W - 1, d_ff))

    scratch += [pltpu.VMEM((rows, d), _BF16), pltpu.VMEM((rows, d_ff), _BF16)]
    if prompt:
        scratch.append(pltpu.VMEM((SUBLANES, d_ff), _F32))
    if kind == 0:
        scratch.append(pltpu.VMEM((rows, d), _BF16))
        if prompt:
            scratch.append(pltpu.VMEM((SUBLANES, d), _F32))
    elif kind == 1:
        scratch.append(pltpu.VMEM((rows, d), _BF16))
        if prompt:
            scratch.append(pltpu.VMEM((rows, d), _BF16))
    else:
        scratch.append(pltpu.VMEM((2 * SUBLANES + rows if prompt else rows, d), _F32))

    res = pl.pallas_call(
        functools.partial(_layer_kernel, kind=kind, prompt=prompt, final_norm=final_norm,
                          step=step, rows=rows),
        grid=grid,
        in_specs=in_specs,
        out_specs=out_specs,
        out_shape=outs,
        scratch_shapes=scratch,
        compiler_params=pltpu.CompilerParams(
            dimension_semantics=("arbitrary",) * len(grid),
            vmem_limit_bytes=VMEM_LIMIT_BYTES),
        name=f"layer{i}_{'prompt' if prompt else 'sample'}",
    )(*ins)
    x_new, ffn_new = res[0], res[-1]
    mix_new = res[1] if len(res) == 3 else None
    return x_new, mix_new, ffn_new


def _trunk(x, conv_prev, pool_prev, ffn_prev, w, *, prompt, group):
    depth = w["g_mix"].shape[0]
    new = {0: [], 1: [], 2: []}
    new_ffn = []
    for i in range(depth):
        kind = i % 3
        mix_state = conv_prev if kind == 0 else pool_prev
        x, mix_new, ffn_new = _layer(x, i, mix_state, ffn_prev, w, prompt=prompt, group=group)
        if mix_new is not None:
            new[kind].append(mix_new)
        new_ffn.append(ffn_new)
    return x, jnp.stack(new[0]), jnp.stack(new[2]), jnp.stack(new_ffn), new[1]


def _to_groups(a, group):
    *lead, b, k, c = a.shape
    a = a.reshape(*lead, b // group, group, k, c)
    a = jnp.swapaxes(a, -3, -2)
    return a.reshape(*lead, b * k, c)


def _from_groups(a, group, k):
    *lead, n, c = a.shape
    a = a.reshape(*lead, n // (k * group), k, group, c)
    a = jnp.swapaxes(a, -3, -2)
    return a.reshape(*lead, n // k, k, c)


def kernel(x_prompt, x_sample, state_shortconv, state_pool, state_ffnconv, g_mix, g_ffn, g_final, a_w_in, a_conv, a_w_out, b_w_in, b_g_v, b_w_s, b_bias, b_w_out, c_w_group, c_scale, f_w_up, f_conv, f_conv_b, f_w_down):
    batch, seq, d = x_prompt.shape
    dec_batch, dec_seq, _ = x_sample.shape
    row = lambda a: a[:, None, :]
    w = dict(g_mix=row(g_mix), g_ffn=row(g_ffn), g_final=g_final[None, :],
             a_w_in=a_w_in.astype(_BF16), a_conv=a_conv, a_w_out=a_w_out.astype(_BF16),
             b_w_in=b_w_in.astype(_BF16), b_g_v=row(b_g_v), b_w_s=b_w_s, b_bias=b_bias,
             b_bias_t=jnp.swapaxes(b_bias, 1, 2), b_w_out=b_w_out.astype(_BF16),
             c_w_group=c_w_group.astype(_BF16), c_scale=row(c_scale),
             f_w_up=f_w_up.astype(_BF16), f_conv=f_conv, f_conv_b=row(f_conv_b),
             f_w_down=f_w_down.astype(_BF16))

    yp, conv_p, pool_p, ffn_p, _ = _trunk(
        x_prompt.reshape(batch * seq, d), None, None, None, w, prompt=True, group=batch)

    group = TILE_ROWS // dec_seq
    tg = functools.partial(_to_groups, group=group)
    ys, conv_s, pool_s, ffn_s, v_s = _trunk(
        tg(x_sample), tg(state_shortconv), tg(state_pool), tg(state_ffnconv), w,
        prompt=False, group=group)
    fg = functools.partial(_from_groups, group=group)

    return (yp.reshape(batch, seq, d),
            fg(ys, k=dec_seq),
            conv_p, fg(conv_s, k=CONV_W - 1),
            pool_p, fg(pool_s, k=POOL_STATE),
            ffn_p, fg(ffn_s, k=CONV_W - 1),
            fg(jnp.stack(v_s), k=dec_seq))
```

```python
import functools

import jax
import jax.numpy as jnp
from jax import lax
from jax.experimental import pallas as pl
from jax.experimental.pallas import tpu as pltpu

EPS = 1e-6
CONV_W = 3
CHUNK = 128
POOL_WINDOWS = (2, 4, 8, 16)
POOL_STATE = max(POOL_WINDOWS) - 1

LANE_TILE = 256
SUBLANES = 8
TILE_ROWS = 512
VMEM_LIMIT_BYTES = 56 * 1024 * 1024

_BF16 = jnp.bfloat16
_F32 = jnp.float32


def _rms(x, g):
    return x * lax.rsqrt(jnp.mean(x * x, axis=-1, keepdims=True) + EPS) * g


def _dot(a, b):
    return jnp.dot(a, b, preferred_element_type=_F32)


def _layer_spec(arr, layer):
    if layer is None:
        blk, idx = arr.shape, (0,) * arr.ndim
    else:
        blk, idx = (None,) + arr.shape[1:], (layer,) + (0,) * (arr.ndim - 1)
    return pl.BlockSpec(blk, lambda *_: idx, pipeline_mode=pl.Buffered(1))


def _big_weights(i):
    kind, j = i % 3, i // 3
    mixer = {0: ("a_w_in", "a_w_out"), 1: ("b_w_in", "b_w_out"), 2: ("c_w_group",)}[kind]
    return [(name, j) for name in mixer] + [("f_w_up", i), ("f_w_down", i)]


def _cast_rows(n_rows, n_steps):
    bf16_rows = 2 * SUBLANES
    if n_rows % (n_steps * bf16_rows) == 0:
        return n_rows // n_steps
    return CHUNK


def _delayed(cur, hist_ref, cols, k, *, step, prompt):
    rows = cur.shape[0]
    if prompt:
        rolled = pltpu.roll(cur, k, 0)
        row = lax.broadcasted_iota(jnp.int32, (SUBLANES, cur.shape[1]), 0)
        top = jnp.where(row < k, pltpu.roll(hist_ref[:, cols], k, 0), rolled[0:SUBLANES, :])
        return jnp.concatenate([top, rolled[SUBLANES:, :]], axis=0)
    n_hist = hist_ref.shape[0]
    return jnp.concatenate([hist_ref[n_hist - k * step:n_hist, cols], cur[0:rows - k * step, :]],
                           axis=0)


def _conv3(cur, hist_ref, cw_ref, cols, *, step, prompt):
    return (cw_ref[0:1, cols] * _delayed(cur, hist_ref, cols, 2, step=step, prompt=prompt)
            + cw_ref[1:2, cols] * _delayed(cur, hist_ref, cols, 1, step=step, prompt=prompt)
            + cw_ref[2:3, cols] * cur)


def _save_tail(cur, hist_ref, st_ref, cols, *, step, prompt):
    rows = cur.shape[0]
    st_ref[:, cols] = cur[rows - (CONV_W - 1) * step:rows, :]
    if prompt:
        hist_ref[:, cols] = cur[rows - SUBLANES:rows, :]


def _sconv_part(x_ref, o_ref, hist_ref, st_ref, gn_ref, win_ref, cw_ref, wout_ref,
                h_s, z_s, *, step, prompt):
    d = x_ref.shape[1]
    h_s[...] = _rms(x_ref[...], gn_ref[...]).astype(_BF16)
    for c in range(d // LANE_TILE):
        cols = slice(c * LANE_TILE, (c + 1) * LANE_TILE)
        bg = _dot(h_s[...], win_ref[:, cols])
        cg = _dot(h_s[...], win_ref[:, d + c * LANE_TILE:d + (c + 1) * LANE_TILE])
        v = _dot(h_s[...], win_ref[:, 2 * d + c * LANE_TILE:2 * d + (c + 1) * LANE_TILE])
        u = cg * v
        y = _conv3(u, hist_ref, cw_ref, cols, step=step, prompt=prompt)
        z_s[:, cols] = (bg * y).astype(_BF16)
        _save_tail(u, hist_ref, st_ref, cols, step=step, prompt=prompt)
    o_ref[...] = x_ref[...] + _dot(z_s[...], wout_ref[...])


def _cmlp_prompt_part(x_ref, o_ref, gn_ref, win_ref, gv_ref, ws_ref, bias_ref, wout_ref,
                      h_s, z_s, vb_s, *, rows):
    d = x_ref.shape[1]
    n_heads = ws_ref.shape[0]
    head_dim = d // n_heads
    h_s[...] = _rms(x_ref[...], gn_ref[...]).astype(_BF16)
    vb_s[...] = _rms(_dot(h_s[...], win_ref[:, d:2 * d]), gv_ref[...]).astype(_BF16)

    t_idx = lax.broadcasted_iota(jnp.int32, (CHUNK, CHUNK), 0)
    s_idx = lax.broadcasted_iota(jnp.int32, (CHUNK, CHUNK), 1)
    causal = t_idx >= s_idx
    for hd in range(n_heads):
        cols = slice(hd * head_dim, (hd + 1) * head_dim)
        u = _dot(h_s[...], win_ref[:, cols])
        w_m = jnp.where(causal, ws_ref[hd], 0.0).astype(_BF16)
        bias = bias_ref[:, hd:hd + 1]
        for n in range(rows // CHUNK):
            rws = slice(n * CHUNK, (n + 1) * CHUNK)
            s = _dot(w_m, vb_s[rws, cols]) + bias
            z_s[rws, cols] = (u[rws, :] * s).astype(_BF16)
    o_ref[...] = x_ref[...] + _dot(z_s[...], wout_ref[...])


def _cmlp_sample_part(x_ref, o_ref, v_ref, wsm_ref, bsm_ref, gn_ref, win_ref, gv_ref, wout_ref,
                      h_s, z_s, *, step, rows):
    d = x_ref.shape[1]
    n_steps = rows // step
    n_heads = bsm_ref.shape[0] // n_steps
    head_dim = d // n_heads
    h_s[...] = _rms(x_ref[...], gn_ref[...]).astype(_BF16)
    v_ref[...] = _rms(_dot(h_s[...], win_ref[:, d:2 * d]), gv_ref[...])
    for hd in range(n_heads):
        cols = slice(hd * head_dim, (hd + 1) * head_dim)
        u = _dot(h_s[...], win_ref[:, cols])
        for t in range(n_steps):
            rws = slice(t * step, (t + 1) * step)
            s = jnp.full((step, head_dim), bsm_ref[hd * n_steps + t], _F32)
            for k in range(t + 1):
                w = wsm_ref[(hd * n_steps + t) * n_steps + k]
                s = s + w * v_ref[k * step:(k + 1) * step, cols]
            z_s[rws, cols] = (u[rws, :] * s).astype(_BF16)
    o_ref[...] = x_ref[...] + _dot(z_s[...], wout_ref[...])


def _pool_prompt_part(x_ref, o_ref, st_ref, gn_ref, wg_ref, sc_ref, hbuf, *, rows):
    hist = 2 * SUBLANES
    gdim = wg_ref.shape[1]
    j = pl.program_id(1)

    @pl.when(j == 0)
    def _():
        hbuf[0:hist, :] = jnp.zeros((hist, hbuf.shape[1]), _F32)

    @pl.when(j > 0)
    def _():
        hbuf[0:hist, :] = hbuf[rows:rows + hist, :]

    hbuf[hist:hist + rows, :] = _rms(x_ref[...], gn_ref[...])
    pos = j * rows + lax.broadcasted_iota(jnp.int32, (rows, 1), 0)
    for gi, w in enumerate(POOL_WINDOWS):
        cols = slice(gi * gdim, (gi + 1) * gdim)
        h = hbuf[hist:hist + rows, cols]
        tot = h
        for k in range(1, w):
            tot = tot + hbuf[hist - k:hist - k + rows, cols]
        cnt = jnp.minimum(pos + 1, w).astype(_F32)
        dlt = (tot / cnt - h).astype(_BF16)
        o_ref[:, cols] = x_ref[:, cols] + _dot(dlt, wg_ref[cols, :]) * sc_ref[:, cols]
    st_ref[...] = hbuf[hist + rows - POOL_STATE:hist + rows, :]


def _pool_sample_part(x_ref, o_ref, state_ref, st_ref, gn_ref, wg_ref, sc_ref, hbuf, *, step, rows):
    hist = POOL_STATE * step
    gdim = wg_ref.shape[1]
    hbuf[...] = _rms(x_ref[...], gn_ref[...])
    for gi, w in enumerate(POOL_WINDOWS):
        cols = slice(gi * gdim, (gi + 1) * gdim)
        h = hbuf[:, cols]
        tot = h
        for k in range(1, w):
            back = k * step
            if back >= rows:
                shifted = state_ref[hist - back:hist - back + rows, cols]
            else:
                shifted = jnp.concatenate(
                    [state_ref[hist - back:hist, cols], hbuf[0:rows - back, cols]], axis=0)
            tot = tot + shifted
        dlt = (tot / float(w) - h).astype(_BF16)
        o_ref[:, cols] = x_ref[:, cols] + _dot(dlt, wg_ref[cols, :]) * sc_ref[:, cols]
    st_ref[0:hist - rows, :] = state_ref[rows:hist, :]
    st_ref[hist - rows:hist, :] = hbuf[...]


def _ffn_part(o_ref, hist_ref, st_ref, gn_ref, wup_ref, cw_ref, cb_ref, wdn_ref, gfin_ref,
              h_s, act_s, *, step, prompt):
    d_ff = wdn_ref.shape[0]
    h_s[...] = _rms(o_ref[...], gn_ref[...]).astype(_BF16)
    for c in range(d_ff // LANE_TILE):
        cols = slice(c * LANE_TILE, (c + 1) * LANE_TILE)
        g = _dot(h_s[...], wup_ref[:, cols])
        a = _dot(h_s[...], wup_ref[:, d_ff + c * LANE_TILE:d_ff + (c + 1) * LANE_TILE])
        y = _conv3(g, hist_ref, cw_ref, cols, step=step, prompt=prompt) + cb_ref[:, cols]
        act_s[:, cols] = (y * jax.nn.sigmoid(y) * a).astype(_BF16)
        _save_tail(g, hist_ref, st_ref, cols, step=step, prompt=prompt)
    out = o_ref[...] + _dot(act_s[...], wdn_ref[...])
    if gfin_ref is not None:
        out = _rms(out, gfin_ref[...])
    o_ref[...] = out


def _layer_kernel(*refs, kind, prompt, final_norm, step, rows, n_cast):
    it = iter(refs)
    nxt = lambda n=1: next(it) if n == 1 else [next(it) for _ in range(n)]
    conv_kw = dict(step=step, prompt=prompt)
    if kind == 1 and not prompt:
        wsm_ref, bsm_ref = nxt(2)
    x_ref = nxt()
    mix_state = nxt() if (not prompt and kind != 1) else None
    ffn_state = nxt() if not prompt else None
    n_mix_w = {0: 4, 1: 6 if prompt else 4, 2: 3}[kind]
    mix_w = nxt(n_mix_w)
    ffn_w = nxt(5)
    gfin_ref = nxt() if final_norm else None
    cast_in = [nxt() for _ in range(n_cast)]
    o_ref = nxt()
    mix_out = nxt() if not (prompt and kind == 1) else None
    ffn_st = nxt()
    cast_out = [nxt() for _ in range(n_cast)]
    h_s, act_s = nxt(2)

    for src, dst in zip(cast_in, cast_out):
        dst[...] = src[...].astype(_BF16)
    if prompt:
        ffn_carry = nxt()

        @pl.when(pl.program_id(1) == 0)
        def _():
            ffn_carry[...] = jnp.zeros(ffn_carry.shape, _F32)
    else:
        ffn_carry = ffn_state

    if kind == 0:
        z_s = nxt()
        if prompt:
            mix_carry = nxt()

            @pl.when(pl.program_id(1) == 0)
            def _():
                mix_carry[...] = jnp.zeros(mix_carry.shape, _F32)
        else:
            mix_carry = mix_state
        _sconv_part(x_ref, o_ref, mix_carry, mix_out, *mix_w, h_s, z_s, **conv_kw)
    elif kind == 1:
        if prompt:
            z_s, vb_s = nxt(2)
            _cmlp_prompt_part(x_ref, o_ref, *mix_w, h_s, z_s, vb_s, rows=rows)
        else:
            z_s = nxt()
            _cmlp_sample_part(x_ref, o_ref, mix_out, wsm_ref, bsm_ref, *mix_w, h_s, z_s,
                              step=step, rows=rows)
    else:
        hbuf = nxt()
        if prompt:
            _pool_prompt_part(x_ref, o_ref, mix_out, *mix_w, hbuf, rows=rows)
        else:
            _pool_sample_part(x_ref, o_ref, mix_state, mix_out, *mix_w, hbuf, step=step, rows=rows)

    _ffn_part(o_ref, ffn_carry, ffn_st, *ffn_w, gfin_ref, h_s, act_s, **conv_kw)


def _layer(x, i, mix_state, ffn_state, w, wb, cast_next, *, prompt, group):
    n, d = x.shape
    d_ff = wb["f_w_down"].shape[0]
    depth = w["g_mix"].shape[0]
    kind, j = i % 3, i // 3
    final_norm = i == depth - 1
    rows = TILE_ROWS
    if prompt:
        n_t = n // group // rows
        grid = (group, n_t)
        step = 1
        x_spec = pl.BlockSpec((rows, d), lambda b, t: (b * n_t + t, 0))

        def state_spec(n_rows, width):
            return pl.BlockSpec((None, n_rows, width), lambda b, t: (b, 0, 0))

        def state_shape(n_rows, width):
            return jax.ShapeDtypeStruct((group, n_rows, width), _F32)
    else:
        grid = (n // rows,)
        step = group
        x_spec = pl.BlockSpec((rows, d), lambda g: (g, 0))

        def state_spec(n_rows, width, layer=None, **kw):
            if layer is None:
                return pl.BlockSpec((n_rows * step, width), lambda g: (g, 0))
            return pl.BlockSpec((None, n_rows * step, width), lambda g: (layer, g, 0), **kw)

        def state_shape(n_rows, width):
            return jax.ShapeDtypeStruct((grid[0] * n_rows * step, width), _F32)

    ins, in_specs, outs, out_specs, scratch = [], [], [], [], []

    def add_in(arr, spec):
        ins.append(arr)
        in_specs.append(spec)

    def add_w(name, layer):
        if name in wb:
            add_in(wb[name], _layer_spec(wb[name], None))
        else:
            add_in(w[name], _layer_spec(w[name], layer))

    if kind == 1 and not prompt:
        n_steps = rows // step
        smem = pl.BlockSpec(memory_space=pltpu.SMEM)
        add_in(w["b_w_s"][j, :, :n_steps, :n_steps].reshape(-1), smem)
        add_in(w["b_bias"][j, :, :n_steps].reshape(-1), smem)
    add_in(x, x_spec)
    if not prompt:
        if kind == 0:
            add_in(mix_state, state_spec(CONV_W - 1, d, j))
        elif kind == 2:
            add_in(mix_state, state_spec(POOL_STATE, d, j, pipeline_mode=pl.Buffered(1)))
        add_in(ffn_state, state_spec(CONV_W - 1, d_ff, i))
    add_w("g_mix", i)
    if kind == 0:
        for name in ("a_w_in", "a_conv", "a_w_out"):
            add_w(name, j)
    elif kind == 1:
        names = ("b_w_in", "b_g_v", "b_w_s", "b_bias_t", "b_w_out") if prompt else (
            "b_w_in", "b_g_v", "b_w_out")
        for name in names:
            add_w(name, j)
    else:
        for name in ("c_w_group", "c_scale"):
            add_w(name, j)
    for name in ("g_ffn", "f_w_up", "f_conv", "f_conv_b", "f_w_down"):
        add_w(name, i)
    if final_norm:
        add_in(w["g_final"], pl.BlockSpec(w["g_final"].shape, lambda *_: (0, 0)))
    cast_specs = []
    for name, stack, idx in cast_next:
        n_rows, width = stack.shape[1:]
        blk_rows = _cast_rows(n_rows, n // rows)
        last = n_rows // blk_rows - 1
        add_in(stack, pl.BlockSpec(
            (None, blk_rows, width),
            lambda b, t, idx=idx, last=last: (idx, jnp.minimum(b * n_t + t, last), 0)))
        cast_specs.append(pl.BlockSpec(
            (blk_rows, width), lambda b, t, last=last: (jnp.minimum(b * n_t + t, last), 0)))

    outs.append(jax.ShapeDtypeStruct((n, d), _F32))
    out_specs.append(x_spec)
    if kind == 0:
        outs.append(state_shape(CONV_W - 1, d))
        out_specs.append(state_spec(CONV_W - 1, d))
    elif kind == 1 and not prompt:
        outs.append(jax.ShapeDtypeStruct((n, d), _F32))
        out_specs.append(x_spec)
    elif kind == 2:
        outs.append(state_shape(POOL_STATE, d))
        out_specs.append(state_spec(POOL_STATE, d))
    outs.append(state_shape(CONV_W - 1, d_ff))
    out_specs.append(state_spec(CONV_W - 1, d_ff))
    n_fixed_outs = len(outs)
    for (name, stack, idx), spec in zip(cast_next, cast_specs):
        outs.append(jax.ShapeDtypeStruct(stack.shape[1:], _BF16))
        out_specs.append(spec)

    scratch += [pltpu.VMEM((rows, d), _BF16), pltpu.VMEM((rows, d_ff), _BF16)]
    if prompt:
        scratch.append(pltpu.VMEM((SUBLANES, d_ff), _F32))
    if kind == 0:
        scratch.append(pltpu.VMEM((rows, d), _BF16))
        if prompt:
            scratch.append(pltpu.VMEM((SUBLANES, d), _F32))
    elif kind == 1:
        scratch.append(pltpu.VMEM((rows, d), _BF16))
        if prompt:
            scratch.append(pltpu.VMEM((rows, d), _BF16))
    else:
        scratch.append(pltpu.VMEM((2 * SUBLANES + rows if prompt else rows, d), _F32))

    res = pl.pallas_call(
        functools.partial(_layer_kernel, kind=kind, prompt=prompt, final_norm=final_norm,
                          step=step, rows=rows, n_cast=len(cast_next)),
        grid=grid,
        in_specs=in_specs,
        out_specs=out_specs,
        out_shape=outs,
        scratch_shapes=scratch,
        compiler_params=pltpu.CompilerParams(
            dimension_semantics=("arbitrary",) * len(grid),
            vmem_limit_bytes=VMEM_LIMIT_BYTES),
        name=f"layer{i}_{'prompt' if prompt else 'sample'}",
    )(*ins)
    x_new, ffn_new = res[0], res[n_fixed_outs - 1]
    mix_new = res[1] if n_fixed_outs == 3 else None
    wb_next = {name: arr for (name, _, _), arr in zip(cast_next, res[n_fixed_outs:])}
    return x_new, mix_new, ffn_new, wb_next


def _trunk(x, conv_prev, pool_prev, ffn_prev, w, wbs, big, *, prompt, group):
    depth = w["g_mix"].shape[0]
    new = {0: [], 1: [], 2: []}
    new_ffn = []
    for i in range(depth):
        kind = i % 3
        mix_state = conv_prev if kind == 0 else pool_prev
        cast_next = []
        if prompt and i + 1 < depth:
            cast_next = [(name, big[name], idx) for name, idx in _big_weights(i + 1)]
        x, mix_new, ffn_new, wb_next = _layer(x, i, mix_state, ffn_prev, w, wbs[i], cast_next,
                                              prompt=prompt, group=group)
        if cast_next:
            wbs.append(wb_next)
        if mix_new is not None:
            new[kind].append(mix_new)
        new_ffn.append(ffn_new)
    return x, jnp.stack(new[0]), jnp.stack(new[2]), jnp.stack(new_ffn), new[1]


def _to_groups(a, group):
    *lead, b, k, c = a.shape
    a = a.reshape(*lead, b // group, group, k, c)
    a = jnp.swapaxes(a, -3, -2)
    return a.reshape(*lead, b * k, c)


def _from_groups(a, group, k):
    *lead, n, c = a.shape
    a = a.reshape(*lead, n // (k * group), k, group, c)
    a = jnp.swapaxes(a, -3, -2)
    return a.reshape(*lead, n // k, k, c)


def kernel(x_prompt, x_sample, state_shortconv, state_pool, state_ffnconv, g_mix, g_ffn, g_final, a_w_in, a_conv, a_w_out, b_w_in, b_g_v, b_w_s, b_bias, b_w_out, c_w_group, c_scale, f_w_up, f_conv, f_conv_b, f_w_down):
    batch, seq, d = x_prompt.shape
    dec_batch, dec_seq, _ = x_sample.shape
    row = lambda a: a[:, None, :]
    w = dict(g_mix=row(g_mix), g_ffn=row(g_ffn), g_final=g_final[None, :],
             a_conv=a_conv, b_g_v=row(b_g_v), b_w_s=b_w_s, b_bias=b_bias,
             b_bias_t=jnp.swapaxes(b_bias, 1, 2), c_scale=row(c_scale),
             f_conv=f_conv, f_conv_b=row(f_conv_b))
    big = dict(a_w_in=a_w_in, a_w_out=a_w_out, b_w_in=b_w_in, b_w_out=b_w_out,
               c_w_group=c_w_group.reshape(c_w_group.shape[0], -1, c_w_group.shape[-1]),
               f_w_up=f_w_up, f_w_down=f_w_down)
    wbs = [{name: big[name][idx].astype(_BF16) for name, idx in _big_weights(0)}]

    yp, conv_p, pool_p, ffn_p, _ = _trunk(
        x_prompt.reshape(batch * seq, d), None, None, None, w, wbs, big, prompt=True, group=batch)

    group = TILE_ROWS // dec_seq
    tg = functools.partial(_to_groups, group=group)
    ys, conv_s, pool_s, ffn_s, v_s = _trunk(
        tg(x_sample), tg(state_shortconv), tg(state_pool), tg(state_ffnconv), w, wbs, big,
        prompt=False, group=group)
    fg = functools.partial(_from_groups, group=group)

    return (yp.reshape(batch, seq, d),
            fg(ys, k=dec_seq),
            conv_p, fg(conv_s, k=CONV_W - 1),
            pool_p, fg(pool_s, k=POOL_STATE),
            ffn_p, fg(ffn_s, k=CONV_W - 1),
            fg(jnp.stack(v_s), k=dec_seq))
```

```python
import functools

import jax
import jax.numpy as jnp
from jax import lax
from jax.experimental import pallas as pl
from jax.experimental.pallas import tpu as pltpu

EPS = 1e-6
CONV_W = 3
CHUNK = 128
POOL_WINDOWS = (2, 4, 8, 16)
POOL_STATE = max(POOL_WINDOWS) - 1

LANE_TILE = 256
SUBLANES = 8
PROMPT_ROWS = 1024
TILE_ROWS = 512
VMEM_LIMIT_BYTES = 60 * 1024 * 1024

_BF16 = jnp.bfloat16
_F32 = jnp.float32


def _rms(x, g):
    return x * lax.rsqrt(jnp.mean(x * x, axis=-1, keepdims=True) + EPS) * g


def _dot(a, b):
    return jnp.dot(a, b, preferred_element_type=_F32)


def _layer_spec(arr, layer):
    if layer is None:
        blk, idx = arr.shape, (0,) * arr.ndim
    else:
        blk, idx = (None,) + arr.shape[1:], (layer,) + (0,) * (arr.ndim - 1)
    return pl.BlockSpec(blk, lambda *_: idx, pipeline_mode=pl.Buffered(1))


def _big_weights(i):
    kind, j = i % 3, i // 3
    mixer = {0: ("a_w_in", "a_w_out"), 1: ("b_w_in", "b_w_out"), 2: ("c_w_group",)}[kind]
    return [(name, j) for name in mixer] + [("f_w_up", i), ("f_w_down", i)]


def _cast_rows(n_rows, n_steps):
    bf16_rows = 2 * SUBLANES
    if n_rows % (n_steps * bf16_rows) == 0:
        return n_rows // n_steps
    return CHUNK


def _delayed(cur, hist_ref, cols, k, *, step, prompt):
    rows = cur.shape[0]
    if prompt:
        rolled = pltpu.roll(cur, k, 0)
        row = lax.broadcasted_iota(jnp.int32, (SUBLANES, cur.shape[1]), 0)
        top = jnp.where(row < k, pltpu.roll(hist_ref[:, cols], k, 0), rolled[0:SUBLANES, :])
        return jnp.concatenate([top, rolled[SUBLANES:, :]], axis=0)
    n_hist = hist_ref.shape[0]
    return jnp.concatenate([hist_ref[n_hist - k * step:n_hist, cols], cur[0:rows - k * step, :]],
                           axis=0)


def _conv3(cur, hist_ref, cw_ref, cols, *, step, prompt):
    return (cw_ref[0:1, cols] * _delayed(cur, hist_ref, cols, 2, step=step, prompt=prompt)
            + cw_ref[1:2, cols] * _delayed(cur, hist_ref, cols, 1, step=step, prompt=prompt)
            + cw_ref[2:3, cols] * cur)


def _save_tail(cur, hist_ref, st_ref, cols, *, step, prompt):
    rows = cur.shape[0]
    st_ref[:, cols] = cur[rows - (CONV_W - 1) * step:rows, :]
    if prompt:
        hist_ref[:, cols] = cur[rows - SUBLANES:rows, :]


def _sconv_part(x_ref, o_ref, hist_ref, st_ref, gn_ref, win_ref, cw_ref, wout_ref,
                h_s, z_s, *, step, prompt):
    d = x_ref.shape[1]
    h_s[...] = _rms(x_ref[...], gn_ref[...]).astype(_BF16)
    for c in range(d // LANE_TILE):
        cols = slice(c * LANE_TILE, (c + 1) * LANE_TILE)
        bg = _dot(h_s[...], win_ref[:, cols])
        cg = _dot(h_s[...], win_ref[:, d + c * LANE_TILE:d + (c + 1) * LANE_TILE])
        v = _dot(h_s[...], win_ref[:, 2 * d + c * LANE_TILE:2 * d + (c + 1) * LANE_TILE])
        u = cg * v
        y = _conv3(u, hist_ref, cw_ref, cols, step=step, prompt=prompt)
        z_s[:, cols] = (bg * y).astype(_BF16)
        _save_tail(u, hist_ref, st_ref, cols, step=step, prompt=prompt)
    o_ref[...] = x_ref[...] + _dot(z_s[...], wout_ref[...])


def _cmlp_prompt_part(x_ref, o_ref, gn_ref, win_ref, gv_ref, ws_ref, bias_ref, wout_ref,
                      h_s, z_s, vb_s, *, rows):
    d = x_ref.shape[1]
    n_heads = ws_ref.shape[0]
    head_dim = d // n_heads
    h_s[...] = _rms(x_ref[...], gn_ref[...]).astype(_BF16)
    vb_s[...] = _rms(_dot(h_s[...], win_ref[:, d:2 * d]), gv_ref[...]).astype(_BF16)

    t_idx = lax.broadcasted_iota(jnp.int32, (CHUNK, CHUNK), 0)
    s_idx = lax.broadcasted_iota(jnp.int32, (CHUNK, CHUNK), 1)
    causal = t_idx >= s_idx
    for hd in range(n_heads):
        cols = slice(hd * head_dim, (hd + 1) * head_dim)
        u = _dot(h_s[...], win_ref[:, cols])
        w_m = jnp.where(causal, ws_ref[hd], 0.0).astype(_BF16)
        bias = bias_ref[:, hd:hd + 1]
        for n in range(rows // CHUNK):
            rws = slice(n * CHUNK, (n + 1) * CHUNK)
            s = _dot(w_m, vb_s[rws, cols]) + bias
            z_s[rws, cols] = (u[rws, :] * s).astype(_BF16)
    o_ref[...] = x_ref[...] + _dot(z_s[...], wout_ref[...])


def _cmlp_sample_part(x_ref, o_ref, v_ref, wsm_ref, bsm_ref, gn_ref, win_ref, gv_ref, wout_ref,
                      h_s, z_s, *, step, rows):
    d = x_ref.shape[1]
    n_steps = rows // step
    n_heads = bsm_ref.shape[0] // n_steps
    head_dim = d // n_heads
    h_s[...] = _rms(x_ref[...], gn_ref[...]).astype(_BF16)
    v_ref[...] = _rms(_dot(h_s[...], win_ref[:, d:2 * d]), gv_ref[...])
    for hd in range(n_heads):
        cols = slice(hd * head_dim, (hd + 1) * head_dim)
        u = _dot(h_s[...], win_ref[:, cols])
        for t in range(n_steps):
            rws = slice(t * step, (t + 1) * step)
            s = jnp.full((step, head_dim), bsm_ref[hd * n_steps + t], _F32)
            for k in range(t + 1):
                w = wsm_ref[(hd * n_steps + t) * n_steps + k]
                s = s + w * v_ref[k * step:(k + 1) * step, cols]
            z_s[rws, cols] = (u[rws, :] * s).astype(_BF16)
    o_ref[...] = x_ref[...] + _dot(z_s[...], wout_ref[...])


def _pool_prompt_part(x_ref, o_ref, st_ref, gn_ref, wg_ref, sc_ref, hbuf, *, rows):
    hist = 2 * SUBLANES
    gdim = wg_ref.shape[1]
    j = pl.program_id(1)

    @pl.when(j == 0)
    def _():
        hbuf[0:hist, :] = jnp.zeros((hist, hbuf.shape[1]), _F32)

    @pl.when(j > 0)
    def _():
        hbuf[0:hist, :] = hbuf[rows:rows + hist, :]

    hbuf[hist:hist + rows, :] = _rms(x_ref[...], gn_ref[...])
    pos = j * rows + lax.broadcasted_iota(jnp.int32, (rows, 1), 0)
    for gi, w in enumerate(POOL_WINDOWS):
        cols = slice(gi * gdim, (gi + 1) * gdim)
        h = hbuf[hist:hist + rows, cols]
        tot = h
        for k in range(1, w):
            tot = tot + hbuf[hist - k:hist - k + rows, cols]
        cnt = jnp.minimum(pos + 1, w).astype(_F32)
        dlt = (tot / cnt - h).astype(_BF16)
        o_ref[:, cols] = x_ref[:, cols] + _dot(dlt, wg_ref[cols, :]) * sc_ref[:, cols]
    st_ref[...] = hbuf[hist + rows - POOL_STATE:hist + rows, :]


def _pool_sample_part(x_ref, o_ref, state_ref, st_ref, gn_ref, wg_ref, sc_ref, hbuf, *, step, rows):
    hist = POOL_STATE * step
    gdim = wg_ref.shape[1]
    hbuf[...] = _rms(x_ref[...], gn_ref[...])
    for gi, w in enumerate(POOL_WINDOWS):
        cols = slice(gi * gdim, (gi + 1) * gdim)
        h = hbuf[:, cols]
        tot = h
        for k in range(1, w):
            back = k * step
            if back >= rows:
                shifted = state_ref[hist - back:hist - back + rows, cols]
            else:
                shifted = jnp.concatenate(
                    [state_ref[hist - back:hist, cols], hbuf[0:rows - back, cols]], axis=0)
            tot = tot + shifted
        dlt = (tot / float(w) - h).astype(_BF16)
        o_ref[:, cols] = x_ref[:, cols] + _dot(dlt, wg_ref[cols, :]) * sc_ref[:, cols]
    st_ref[0:hist - rows, :] = state_ref[rows:hist, :]
    st_ref[hist - rows:hist, :] = hbuf[...]


def _ffn_part(o_ref, hist_ref, st_ref, gn_ref, wup_ref, cw_ref, cb_ref, wdn_ref, gfin_ref,
              h_s, act_s, *, step, prompt):
    d_ff = wdn_ref.shape[0]
    h_s[...] = _rms(o_ref[...], gn_ref[...]).astype(_BF16)
    for c in range(d_ff // LANE_TILE):
        cols = slice(c * LANE_TILE, (c + 1) * LANE_TILE)
        g = _dot(h_s[...], wup_ref[:, cols])
        a = _dot(h_s[...], wup_ref[:, d_ff + c * LANE_TILE:d_ff + (c + 1) * LANE_TILE])
        y = _conv3(g, hist_ref, cw_ref, cols, step=step, prompt=prompt) + cb_ref[:, cols]
        act_s[:, cols] = (y * jax.nn.sigmoid(y) * a).astype(_BF16)
        _save_tail(g, hist_ref, st_ref, cols, step=step, prompt=prompt)
    out = o_ref[...] + _dot(act_s[...], wdn_ref[...])
    if gfin_ref is not None:
        out = _rms(out, gfin_ref[...])
    o_ref[...] = out


def _layer_kernel(*refs, kind, prompt, final_norm, step, rows, n_cast):
    it = iter(refs)
    nxt = lambda n=1: next(it) if n == 1 else [next(it) for _ in range(n)]
    conv_kw = dict(step=step, prompt=prompt)
    if kind == 1 and not prompt:
        wsm_ref, bsm_ref = nxt(2)
    x_ref = nxt()
    mix_state = nxt() if (not prompt and kind != 1) else None
    ffn_state = nxt() if not prompt else None
    n_mix_w = {0: 4, 1: 6 if prompt else 4, 2: 3}[kind]
    mix_w = nxt(n_mix_w)
    ffn_w = nxt(5)
    gfin_ref = nxt() if final_norm else None
    cast_in = [nxt() for _ in range(n_cast)]
    o_ref = nxt()
    mix_out = nxt() if not (prompt and kind == 1) else None
    ffn_st = nxt()
    cast_out = [nxt() for _ in range(n_cast)]
    h_s, act_s = nxt(2)

    for src, dst in zip(cast_in, cast_out):
        dst[...] = src[...].astype(_BF16)
    if prompt:
        ffn_carry = nxt()

        @pl.when(pl.program_id(1) == 0)
        def _():
            ffn_carry[...] = jnp.zeros(ffn_carry.shape, _F32)
    else:
        ffn_carry = ffn_state

    if kind == 0:
        z_s = nxt()
        if prompt:
            mix_carry = nxt()

            @pl.when(pl.program_id(1) == 0)
            def _():
                mix_carry[...] = jnp.zeros(mix_carry.shape, _F32)
        else:
            mix_carry = mix_state
        _sconv_part(x_ref, o_ref, mix_carry, mix_out, *mix_w, h_s, z_s, **conv_kw)
    elif kind == 1:
        if prompt:
            z_s, vb_s = nxt(2)
            _cmlp_prompt_part(x_ref, o_ref, *mix_w, h_s, z_s, vb_s, rows=rows)
        else:
            z_s = nxt()
            _cmlp_sample_part(x_ref, o_ref, mix_out, wsm_ref, bsm_ref, *mix_w, h_s, z_s,
                              step=step, rows=rows)
    else:
        hbuf = nxt()
        if prompt:
            _pool_prompt_part(x_ref, o_ref, mix_out, *mix_w, hbuf, rows=rows)
        else:
            _pool_sample_part(x_ref, o_ref, mix_state, mix_out, *mix_w, hbuf, step=step, rows=rows)

    _ffn_part(o_ref, ffn_carry, ffn_st, *ffn_w, gfin_ref, h_s, act_s, **conv_kw)


def _layer(x, i, mix_state, ffn_state, w, wb, cast_next, *, prompt, group):
    n, d = x.shape
    d_ff = wb["f_w_down"].shape[0]
    depth = w["g_mix"].shape[0]
    kind, j = i % 3, i // 3
    final_norm = i == depth - 1
    rows = PROMPT_ROWS if prompt else TILE_ROWS
    if prompt:
        n_t = n // group // rows
        grid = (group, n_t)
        step = 1
        x_spec = pl.BlockSpec((rows, d), lambda b, t: (b * n_t + t, 0))

        def state_spec(n_rows, width):
            return pl.BlockSpec((None, n_rows, width), lambda b, t: (b, 0, 0))

        def state_shape(n_rows, width):
            return jax.ShapeDtypeStruct((group, n_rows, width), _F32)
    else:
        grid = (n // rows,)
        step = group
        x_spec = pl.BlockSpec((rows, d), lambda g: (g, 0))

        def state_spec(n_rows, width, layer=None, **kw):
            if layer is None:
                return pl.BlockSpec((n_rows * step, width), lambda g: (g, 0))
            return pl.BlockSpec((None, n_rows * step, width), lambda g: (layer, g, 0), **kw)

        def state_shape(n_rows, width):
            return jax.ShapeDtypeStruct((grid[0] * n_rows * step, width), _F32)

    ins, in_specs, outs, out_specs, scratch = [], [], [], [], []

    def add_in(arr, spec):
        ins.append(arr)
        in_specs.append(spec)

    def add_w(name, layer):
        if name in wb:
            add_in(wb[name], _layer_spec(wb[name], None))
        else:
            add_in(w[name], _layer_spec(w[name], layer))

    if kind == 1 and not prompt:
        n_steps = rows // step
        smem = pl.BlockSpec(memory_space=pltpu.SMEM)
        add_in(w["b_w_s"][j, :, :n_steps, :n_steps].reshape(-1), smem)
        add_in(w["b_bias"][j, :, :n_steps].reshape(-1), smem)
    add_in(x, x_spec)
    if not prompt:
        if kind == 0:
            add_in(mix_state, state_spec(CONV_W - 1, d, j))
        elif kind == 2:
            add_in(mix_state, state_spec(POOL_STATE, d, j, pipeline_mode=pl.Buffered(1)))
        add_in(ffn_state, state_spec(CONV_W - 1, d_ff, i))
    add_w("g_mix", i)
    if kind == 0:
        for name in ("a_w_in", "a_conv", "a_w_out"):
            add_w(name, j)
    elif kind == 1:
        names = ("b_w_in", "b_g_v", "b_w_s", "b_bias_t", "b_w_out") if prompt else (
            "b_w_in", "b_g_v", "b_w_out")
        for name in names:
            add_w(name, j)
    else:
        for name in ("c_w_group", "c_scale"):
            add_w(name, j)
    for name in ("g_ffn", "f_w_up", "f_conv", "f_conv_b", "f_w_down"):
        add_w(name, i)
    if final_norm:
        add_in(w["g_final"], pl.BlockSpec(w["g_final"].shape, lambda *_: (0, 0)))
    cast_specs = []
    for name, stack, idx in cast_next:
        n_rows, width = stack.shape[1:]
        blk_rows = _cast_rows(n_rows, n // rows)
        last = n_rows // blk_rows - 1
        add_in(stack, pl.BlockSpec(
            (None, blk_rows, width),
            lambda b, t, idx=idx, last=last: (idx, jnp.minimum(b * n_t + t, last), 0)))
        cast_specs.append(pl.BlockSpec(
            (blk_rows, width), lambda b, t, last=last: (jnp.minimum(b * n_t + t, last), 0)))

    outs.append(jax.ShapeDtypeStruct((n, d), _F32))
    out_specs.append(x_spec)
    if kind == 0:
        outs.append(state_shape(CONV_W - 1, d))
        out_specs.append(state_spec(CONV_W - 1, d))
    elif kind == 1 and not prompt:
        outs.append(jax.ShapeDtypeStruct((n, d), _F32))
        out_specs.append(x_spec)
    elif kind == 2:
        outs.append(state_shape(POOL_STATE, d))
        out_specs.append(state_spec(POOL_STATE, d))
    outs.append(state_shape(CONV_W - 1, d_ff))
    out_specs.append(state_spec(CONV_W - 1, d_ff))
    n_fixed_outs = len(outs)
    for (name, stack, idx), spec in zip(cast_next, cast_specs):
        outs.append(jax.ShapeDtypeStruct(stack.shape[1:], _BF16))
        out_specs.append(spec)

    scratch += [pltpu.VMEM((rows, d), _BF16), pltpu.VMEM((rows, d_ff), _BF16)]
    if prompt:
        scratch.append(pltpu.VMEM((SUBLANES, d_ff), _F32))
    if kind == 0:
        scratch.append(pltpu.VMEM((rows, d), _BF16))
        if prompt:
            scratch.append(pltpu.VMEM((SUBLANES, d), _F32))
    elif kind == 1:
        scratch.append(pltpu.VMEM((rows, d), _BF16))
        if prompt:
            scratch.append(pltpu.VMEM((rows, d), _BF16))
    else:
        scratch.append(pltpu.VMEM((2 * SUBLANES + rows if prompt else rows, d), _F32))

    res = pl.pallas_call(
        functools.partial(_layer_kernel, kind=kind, prompt=prompt, final_norm=final_norm,
                          step=step, rows=rows, n_cast=len(cast_next)),
        grid=grid,
        in_specs=in_specs,
        out_specs=out_specs,
        out_shape=outs,
        scratch_shapes=scratch,
        compiler_params=pltpu.CompilerParams(
            dimension_semantics=("arbitrary",) * len(grid),
            vmem_limit_bytes=VMEM_LIMIT_BYTES),
        name=f"layer{i}_{'prompt' if prompt else 'sample'}",
    )(*ins)
    x_new, ffn_new = res[0], res[n_fixed_outs - 1]
    mix_new = res[1] if n_fixed_outs == 3 else None
    wb_next = {name: arr for (name, _, _), arr in zip(cast_next, res[n_fixed_outs:])}
    return x_new, mix_new, ffn_new, wb_next


def _trunk(x, conv_prev, pool_prev, ffn_prev, w, wbs, big, *, prompt, group):
    depth = w["g_mix"].shape[0]
    new = {0: [], 1: [], 2: []}
    new_ffn = []
    for i in range(depth):
        kind = i % 3
        mix_state = conv_prev if kind == 0 else pool_prev
        cast_next = []
        if prompt and i + 1 < depth:
            cast_next = [(name, big[name], idx) for name, idx in _big_weights(i + 1)]
        x, mix_new, ffn_new, wb_next = _layer(x, i, mix_state, ffn_prev, w, wbs[i], cast_next,
                                              prompt=prompt, group=group)
        if cast_next:
            wbs.append(wb_next)
        if mix_new is not None:
            new[kind].append(mix_new)
        new_ffn.append(ffn_new)
    return x, jnp.stack(new[0]), jnp.stack(new[2]), jnp.stack(new_ffn), new[1]


def _to_groups(a, group):
    *lead, b, k, c = a.shape
    a = a.reshape(*lead, b // group, group, k, c)
    a = jnp.swapaxes(a, -3, -2)
    return a.reshape(*lead, b * k, c)


def _from_groups(a, group, k):
    *lead, n, c = a.shape
    a = a.reshape(*lead, n // (k * group), k, group, c)
    a = jnp.swapaxes(a, -3, -2)
    return a.reshape(*lead, n // k, k, c)


def kernel(x_prompt, x_sample, state_shortconv, state_pool, state_ffnconv, g_mix, g_ffn, g_final, a_w_in, a_conv, a_w_out, b_w_in, b_g_v, b_w_s, b_bias, b_w_out, c_w_group, c_scale, f_w_up, f_conv, f_conv_b, f_w_down):
    batch, seq, d = x_prompt.shape
    dec_batch, dec_seq, _ = x_sample.shape
    row = lambda a: a[:, None, :]
    w = dict(g_mix=row(g_mix), g_ffn=row(g_ffn), g_final=g_final[None, :],
             a_conv=a_conv, b_g_v=row(b_g_v), b_w_s=b_w_s, b_bias=b_bias,
             b_bias_t=jnp.swapaxes(b_bias, 1, 2), c_scale=row(c_scale),
             f_conv=f_conv, f_conv_b=row(f_conv_b))
    big = dict(a_w_in=a_w_in, a_w_out=a_w_out, b_w_in=b_w_in, b_w_out=b_w_out,
               c_w_group=c_w_group.reshape(c_w_group.shape[0], -1, c_w_group.shape[-1]),
               f_w_up=f_w_up, f_w_down=f_w_down)
    wbs = [{name: big[name][idx].astype(_BF16) for name, idx in _big_weights(0)}]

    yp, conv_p, pool_p, ffn_p, _ = _trunk(
        x_prompt.reshape(batch * seq, d), None, None, None, w, wbs, big, prompt=True, group=batch)

    group = TILE_ROWS // dec_seq
    tg = functools.partial(_to_groups, group=group)
    ys, conv_s, pool_s, ffn_s, v_s = _trunk(
        tg(x_sample), tg(state_shortconv), tg(state_pool), tg(state_ffnconv), w, wbs, big,
        prompt=False, group=group)
    fg = functools.partial(_from_groups, group=group)

    return (yp.reshape(batch, seq, d),
            fg(ys, k=dec_seq),
            conv_p, fg(conv_s, k=CONV_W - 1),
            pool_p, fg(pool_s, k=POOL_STATE),
            ffn_p, fg(ffn_s, k=CONV_W - 1),
            fg(jnp.stack(v_s), k=dec_seq))
```

```python
import functools

import jax
import jax.numpy as jnp
from jax import lax
from jax.experimental import pallas as pl
from jax.experimental.pallas import tpu as pltpu

EPS = 1e-6
CONV_W = 3
CHUNK = 128
POOL_WINDOWS = (2, 4, 8, 16)
POOL_STATE = max(POOL_WINDOWS) - 1

LANE_TILE = 256
SUBLANES = 8
PROMPT_ROWS = 1024
TILE_ROWS = 512
VMEM_LIMIT_BYTES = 60 * 1024 * 1024

_BF16 = jnp.bfloat16
_F32 = jnp.float32


def _rms(x, g):
    return x * lax.rsqrt(jnp.mean(x * x, axis=-1, keepdims=True) + EPS) * g


def _dot(a, b):
    return jnp.dot(a, b, preferred_element_type=_F32)


def _layer_spec(arr, layer):
    if layer is None:
        blk, idx = arr.shape, (0,) * arr.ndim
    else:
        blk, idx = (None,) + arr.shape[1:], (layer,) + (0,) * (arr.ndim - 1)
    return pl.BlockSpec(blk, lambda *_: idx, pipeline_mode=pl.Buffered(1))


def _big_weights(i):
    kind, j = i % 3, i // 3
    mixer = {0: ("a_w_in", "a_w_out"), 1: ("b_w_in", "b_w_out"), 2: ("c_w_group",)}[kind]
    return [(name, j) for name in mixer] + [("f_w_up", i), ("f_w_down", i)]


def _cast_rows(n_rows, n_steps):
    bf16_rows = 2 * SUBLANES
    if n_rows % (n_steps * bf16_rows) == 0:
        return n_rows // n_steps
    return CHUNK


def _delayed(cur, hist_ref, cols, k, *, step, prompt):
    rows = cur.shape[0]
    if prompt:
        rolled = pltpu.roll(cur, k, 0)
        row = lax.broadcasted_iota(jnp.int32, (SUBLANES, cur.shape[1]), 0)
        top = jnp.where(row < k, pltpu.roll(hist_ref[:, cols], k, 0), rolled[0:SUBLANES, :])
        return jnp.concatenate([top, rolled[SUBLANES:, :]], axis=0)
    past = [h[:, cols] for h in hist_ref[len(hist_ref) - k:]]
    return jnp.concatenate(past + [cur[0:rows - k * step, :]], axis=0)


def _conv3(cur, hist_ref, cw_ref, cols, *, step, prompt):
    return (cw_ref[0:1, cols] * _delayed(cur, hist_ref, cols, 2, step=step, prompt=prompt)
            + cw_ref[1:2, cols] * _delayed(cur, hist_ref, cols, 1, step=step, prompt=prompt)
            + cw_ref[2:3, cols] * cur)


def _save_tail(cur, hist_ref, st_ref, cols, *, step, prompt):
    rows = cur.shape[0]
    st_ref[:, cols] = cur[rows - (CONV_W - 1) * step:rows, :]
    if prompt:
        hist_ref[:, cols] = cur[rows - SUBLANES:rows, :]


def _sconv_part(x_ref, o_ref, hist_ref, st_ref, gn_ref, win_ref, cw_ref, wout_ref,
                h_s, z_s, *, step, prompt):
    d = x_ref.shape[1]
    h_s[...] = _rms(x_ref[...], gn_ref[...]).astype(_BF16)
    for c in range(d // LANE_TILE):
        cols = slice(c * LANE_TILE, (c + 1) * LANE_TILE)
        bg = _dot(h_s[...], win_ref[:, cols])
        cg = _dot(h_s[...], win_ref[:, d + c * LANE_TILE:d + (c + 1) * LANE_TILE])
        v = _dot(h_s[...], win_ref[:, 2 * d + c * LANE_TILE:2 * d + (c + 1) * LANE_TILE])
        u = cg * v
        y = _conv3(u, hist_ref, cw_ref, cols, step=step, prompt=prompt)
        z_s[:, cols] = (bg * y).astype(_BF16)
        _save_tail(u, hist_ref, st_ref, cols, step=step, prompt=prompt)
    o_ref[...] = x_ref[...] + _dot(z_s[...], wout_ref[...])


def _cmlp_prompt_part(x_ref, o_ref, gn_ref, win_ref, gv_ref, ws_ref, bias_ref, wout_ref,
                      h_s, z_s, vb_s, *, rows):
    d = x_ref.shape[1]
    n_heads = ws_ref.shape[0]
    head_dim = d // n_heads
    h_s[...] = _rms(x_ref[...], gn_ref[...]).astype(_BF16)
    vb_s[...] = _rms(_dot(h_s[...], win_ref[:, d:2 * d]), gv_ref[...]).astype(_BF16)

    t_idx = lax.broadcasted_iota(jnp.int32, (CHUNK, CHUNK), 0)
    s_idx = lax.broadcasted_iota(jnp.int32, (CHUNK, CHUNK), 1)
    causal = t_idx >= s_idx
    for hd in range(n_heads):
        cols = slice(hd * head_dim, (hd + 1) * head_dim)
        u = _dot(h_s[...], win_ref[:, cols])
        w_m = jnp.where(causal, ws_ref[hd], 0.0).astype(_BF16)
        bias = bias_ref[:, hd:hd + 1]
        for n in range(rows // CHUNK):
            rws = slice(n * CHUNK, (n + 1) * CHUNK)
            s = _dot(w_m, vb_s[rws, cols]) + bias
            z_s[rws, cols] = (u[rws, :] * s).astype(_BF16)
    o_ref[...] = x_ref[...] + _dot(z_s[...], wout_ref[...])


def _cmlp_sample_part(x_ref, o_ref, v_ref, wsm_ref, bsm_ref, gn_ref, win_ref, gv_ref, wout_ref,
                      h_s, z_s, *, step, rows):
    d = x_ref.shape[1]
    n_steps = rows // step
    n_heads = bsm_ref.shape[0] // n_steps
    head_dim = d // n_heads
    h_s[...] = _rms(x_ref[...], gn_ref[...]).astype(_BF16)
    v_ref[...] = _rms(_dot(h_s[...], win_ref[:, d:2 * d]), gv_ref[...])
    for hd in range(n_heads):
        cols = slice(hd * head_dim, (hd + 1) * head_dim)
        u = _dot(h_s[...], win_ref[:, cols])
        for t in range(n_steps):
            rws = slice(t * step, (t + 1) * step)
            s = jnp.full((step, head_dim), bsm_ref[hd * n_steps + t], _F32)
            for k in range(t + 1):
                w = wsm_ref[(hd * n_steps + t) * n_steps + k]
                s = s + w * v_ref[k * step:(k + 1) * step, cols]
            z_s[rws, cols] = (u[rws, :] * s).astype(_BF16)
    o_ref[...] = x_ref[...] + _dot(z_s[...], wout_ref[...])


def _pool_prompt_part(x_ref, o_ref, st_ref, gn_ref, wg_ref, sc_ref, hbuf, *, rows):
    hist = 2 * SUBLANES
    gdim = wg_ref.shape[1]
    j = pl.program_id(1)

    @pl.when(j == 0)
    def _():
        hbuf[0:hist, :] = jnp.zeros((hist, hbuf.shape[1]), _F32)

    @pl.when(j > 0)
    def _():
        hbuf[0:hist, :] = hbuf[rows:rows + hist, :]

    hbuf[hist:hist + rows, :] = _rms(x_ref[...], gn_ref[...])
    pos = j * rows + lax.broadcasted_iota(jnp.int32, (rows, 1), 0)
    for gi, w in enumerate(POOL_WINDOWS):
        cols = slice(gi * gdim, (gi + 1) * gdim)
        h = hbuf[hist:hist + rows, cols]
        tot = h
        for k in range(1, w):
            tot = tot + hbuf[hist - k:hist - k + rows, cols]
        cnt = jnp.minimum(pos + 1, w).astype(_F32)
        dlt = (tot / cnt - h).astype(_BF16)
        o_ref[:, cols] = x_ref[:, cols] + _dot(dlt, wg_ref[cols, :]) * sc_ref[:, cols]
    st_ref[...] = hbuf[hist + rows - POOL_STATE:hist + rows, :]


def _pool_sample_part(x_ref, o_ref, state_ref, st_ref, gn_ref, wg_ref, sc_ref, hbuf, *, step, rows):
    n_state = len(state_ref)
    n_new = rows // step
    gdim = wg_ref.shape[1]
    hbuf[...] = _rms(x_ref[...], gn_ref[...])
    for gi, w in enumerate(POOL_WINDOWS):
        cols = slice(gi * gdim, (gi + 1) * gdim)
        h = hbuf[:, cols]
        tot = h
        for k in range(1, w):
            past = [s[:, cols] for s in state_ref[n_state - k:n_state - k + n_new]]
            if k < n_new:
                past.append(hbuf[0:rows - k * step, cols])
            tot = tot + jnp.concatenate(past, axis=0)
        dlt = (tot / float(w) - h).astype(_BF16)
        o_ref[:, cols] = x_ref[:, cols] + _dot(dlt, wg_ref[cols, :]) * sc_ref[:, cols]
    for t in range(n_state - n_new):
        st_ref[t * step:(t + 1) * step, :] = state_ref[n_new + t][...]
    st_ref[(n_state - n_new) * step:n_state * step, :] = hbuf[...]


def _ffn_part(o_ref, hist_ref, st_ref, gn_ref, wup_ref, cw_ref, cb_ref, wdn_ref, gfin_ref,
              h_s, act_s, *, step, prompt):
    d_ff = wdn_ref.shape[0]
    h_s[...] = _rms(o_ref[...], gn_ref[...]).astype(_BF16)
    for c in range(d_ff // LANE_TILE):
        cols = slice(c * LANE_TILE, (c + 1) * LANE_TILE)
        g = _dot(h_s[...], wup_ref[:, cols])
        a = _dot(h_s[...], wup_ref[:, d_ff + c * LANE_TILE:d_ff + (c + 1) * LANE_TILE])
        y = _conv3(g, hist_ref, cw_ref, cols, step=step, prompt=prompt) + cb_ref[:, cols]
        act_s[:, cols] = (y * jax.nn.sigmoid(y) * a).astype(_BF16)
        _save_tail(g, hist_ref, st_ref, cols, step=step, prompt=prompt)
    out = o_ref[...] + _dot(act_s[...], wdn_ref[...])
    if gfin_ref is not None:
        out = _rms(out, gfin_ref[...])
    o_ref[...] = out


def _layer_kernel(*refs, kind, prompt, final_norm, step, rows, n_cast, x_by_step):
    it = iter(refs)
    nxt = lambda n=1: next(it) if n == 1 else [next(it) for _ in range(n)]
    conv_kw = dict(step=step, prompt=prompt)
    if kind == 1 and not prompt:
        wsm_ref, bsm_ref = nxt(2)
    x_steps = []
    if prompt:
        x_ref = nxt()
    else:
        if x_by_step:
            x_steps = [nxt() for _ in range(rows // step)]
        else:
            x_ref = nxt()
        n_mix_state = {0: CONV_W - 1, 1: 0, 2: POOL_STATE}[kind]
        mix_state = [nxt() for _ in range(n_mix_state)]
        ffn_state = [nxt() for _ in range(CONV_W - 1)]
    n_mix_w = {0: 4, 1: 6 if prompt else 4, 2: 3}[kind]
    mix_w = nxt(n_mix_w)
    ffn_w = nxt(5)
    gfin_ref = nxt() if final_norm else None
    cast_in = [nxt() for _ in range(n_cast)]
    o_ref = nxt()
    mix_out = nxt() if not (prompt and kind == 1) else None
    ffn_st = nxt()
    cast_out = [nxt() for _ in range(n_cast)]
    h_s, act_s = nxt(2)

    for src, dst in zip(cast_in, cast_out):
        dst[...] = src[...].astype(_BF16)
    if prompt:
        ffn_carry = nxt()

        @pl.when(pl.program_id(1) == 0)
        def _():
            ffn_carry[...] = jnp.zeros(ffn_carry.shape, _F32)
    else:
        ffn_carry = ffn_state
        if x_by_step:
            x_ref = nxt()
            for t, x_t in enumerate(x_steps):
                x_ref[t * step:(t + 1) * step, :] = x_t[...]

    if kind == 0:
        z_s = nxt()
        if prompt:
            mix_carry = nxt()

            @pl.when(pl.program_id(1) == 0)
            def _():
                mix_carry[...] = jnp.zeros(mix_carry.shape, _F32)
        else:
            mix_carry = mix_state
        _sconv_part(x_ref, o_ref, mix_carry, mix_out, *mix_w, h_s, z_s, **conv_kw)
    elif kind == 1:
        if prompt:
            z_s, vb_s = nxt(2)
            _cmlp_prompt_part(x_ref, o_ref, *mix_w, h_s, z_s, vb_s, rows=rows)
        else:
            z_s = nxt()
            _cmlp_sample_part(x_ref, o_ref, mix_out, wsm_ref, bsm_ref, *mix_w, h_s, z_s,
                              step=step, rows=rows)
    else:
        hbuf = nxt()
        if prompt:
            _pool_prompt_part(x_ref, o_ref, mix_out, *mix_w, hbuf, rows=rows)
        else:
            _pool_sample_part(x_ref, o_ref, mix_state, mix_out, *mix_w, hbuf, step=step, rows=rows)

    _ffn_part(o_ref, ffn_carry, ffn_st, *ffn_w, gfin_ref, h_s, act_s, **conv_kw)


def _layer(x, i, mix_state, ffn_state, w, wb, cast_next, *, prompt, group):
    d = x.shape[-1]
    n = x.size // d
    d_ff = wb["f_w_down"].shape[0]
    depth = w["g_mix"].shape[0]
    kind, j = i % 3, i // 3
    final_norm = i == depth - 1
    rows = PROMPT_ROWS if prompt else TILE_ROWS
    if prompt:
        n_t = n // group // rows
        grid = (group, n_t)
        step = 1
        x_spec = pl.BlockSpec((rows, d), lambda b, t: (b * n_t + t, 0))

        def state_spec(n_rows, width):
            return pl.BlockSpec((None, n_rows, width), lambda b, t: (b, 0, 0))

        def state_shape(n_rows, width):
            return jax.ShapeDtypeStruct((group, n_rows, width), _F32)
    else:
        grid = (n // rows,)
        step = group
        x_spec = pl.BlockSpec((rows, d), lambda g: (g, 0))

        def state_spec(n_rows, width):
            return pl.BlockSpec((n_rows * step, width), lambda g: (g, 0))

        def state_shape(n_rows, width):
            return jax.ShapeDtypeStruct((grid[0] * n_rows * step, width), _F32)

        def add_steps(arr, lead):
            *outer, n_k, width = arr.shape
            flat = arr.reshape(*outer, n_k * width)
            blk = (None,) * len(lead) + (group, width)
            for k in range(n_k):
                add_in(flat, pl.BlockSpec(blk, lambda g, k=k: lead + (g, k)))

    ins, in_specs, outs, out_specs, scratch = [], [], [], [], []

    def add_in(arr, spec):
        ins.append(arr)
        in_specs.append(spec)

    def add_w(name, layer):
        if name in wb:
            add_in(wb[name], _layer_spec(wb[name], None))
        else:
            add_in(w[name], _layer_spec(w[name], layer))

    if kind == 1 and not prompt:
        n_steps = rows // step
        smem = pl.BlockSpec(memory_space=pltpu.SMEM)
        add_in(w["b_w_s"][j, :, :n_steps, :n_steps].reshape(-1), smem)
        add_in(w["b_bias"][j, :, :n_steps].reshape(-1), smem)
    x_by_step = x.ndim == 3
    if not x_by_step:
        add_in(x, x_spec)
    else:
        add_steps(x, ())
    if not prompt:
        if kind != 1:
            add_steps(mix_state, (j,))
        add_steps(ffn_state, (i,))
    add_w("g_mix", i)
    if kind == 0:
        for name in ("a_w_in", "a_conv", "a_w_out"):
            add_w(name, j)
    elif kind == 1:
        names = ("b_w_in", "b_g_v", "b_w_s", "b_bias_t", "b_w_out") if prompt else (
            "b_w_in", "b_g_v", "b_w_out")
        for name in names:
            add_w(name, j)
    else:
        for name in ("c_w_group", "c_scale"):
            add_w(name, j)
    for name in ("g_ffn", "f_w_up", "f_conv", "f_conv_b", "f_w_down"):
        add_w(name, i)
    if final_norm:
        add_in(w["g_final"], pl.BlockSpec(w["g_final"].shape, lambda *_: (0, 0)))
    cast_specs = []
    for name, stack, idx in cast_next:
        n_rows, width = stack.shape[1:]
        blk_rows = _cast_rows(n_rows, n // rows)
        last = n_rows // blk_rows - 1
        add_in(stack, pl.BlockSpec(
            (None, blk_rows, width),
            lambda b, t, idx=idx, last=last: (idx, jnp.minimum(b * n_t + t, last), 0)))
        cast_specs.append(pl.BlockSpec(
            (blk_rows, width), lambda b, t, last=last: (jnp.minimum(b * n_t + t, last), 0)))

    outs.append(jax.ShapeDtypeStruct((n, d), _F32))
    out_specs.append(x_spec)
    if kind == 0:
        outs.append(state_shape(CONV_W - 1, d))
        out_specs.append(state_spec(CONV_W - 1, d))
    elif kind == 1 and not prompt:
        outs.append(jax.ShapeDtypeStruct((n, d), _F32))
        out_specs.append(x_spec)
    elif kind == 2:
        outs.append(state_shape(POOL_STATE, d))
        out_specs.append(state_spec(POOL_STATE, d))
    outs.append(state_shape(CONV_W - 1, d_ff))
    out_specs.append(state_spec(CONV_W - 1, d_ff))
    n_fixed_outs = len(outs)
    for (name, stack, idx), spec in zip(cast_next, cast_specs):
        outs.append(jax.ShapeDtypeStruct(stack.shape[1:], _BF16))
        out_specs.append(spec)

    scratch += [pltpu.VMEM((rows, d), _BF16), pltpu.VMEM((rows, d_ff), _BF16)]
    if prompt:
        scratch.append(pltpu.VMEM((SUBLANES, d_ff), _F32))
    elif x_by_step:
        scratch.append(pltpu.VMEM((rows, d), _F32))
    if kind == 0:
        scratch.append(pltpu.VMEM((rows, d), _BF16))
        if prompt:
            scratch.append(pltpu.VMEM((SUBLANES, d), _F32))
    elif kind == 1:
        scratch.append(pltpu.VMEM((rows, d), _BF16))
        if prompt:
            scratch.append(pltpu.VMEM((rows, d), _BF16))
    else:
        scratch.append(pltpu.VMEM((2 * SUBLANES + rows if prompt else rows, d), _F32))

    res = pl.pallas_call(
        functools.partial(_layer_kernel, kind=kind, prompt=prompt, final_norm=final_norm,
                          step=step, rows=rows, n_cast=len(cast_next), x_by_step=x_by_step),
        grid=grid,
        in_specs=in_specs,
        out_specs=out_specs,
        out_shape=outs,
        scratch_shapes=scratch,
        compiler_params=pltpu.CompilerParams(
            dimension_semantics=("arbitrary",) * len(grid),
            vmem_limit_bytes=VMEM_LIMIT_BYTES),
        name=f"layer{i}_{'prompt' if prompt else 'sample'}",
    )(*ins)
    x_new, ffn_new = res[0], res[n_fixed_outs - 1]
    mix_new = res[1] if n_fixed_outs == 3 else None
    wb_next = {name: arr for (name, _, _), arr in zip(cast_next, res[n_fixed_outs:])}
    return x_new, mix_new, ffn_new, wb_next


def _trunk(x, conv_prev, pool_prev, ffn_prev, w, wbs, big, *, prompt, group):
    depth = w["g_mix"].shape[0]
    new = {0: [], 1: [], 2: []}
    new_ffn = []
    for i in range(depth):
        kind = i % 3
        mix_state = conv_prev if kind == 0 else pool_prev
        cast_next = []
        if prompt and i + 1 < depth:
            cast_next = [(name, big[name], idx) for name, idx in _big_weights(i + 1)]
        x, mix_new, ffn_new, wb_next = _layer(x, i, mix_state, ffn_prev, w, wbs[i], cast_next,
                                              prompt=prompt, group=group)
        if cast_next:
            wbs.append(wb_next)
        if mix_new is not None:
            new[kind].append(mix_new)
        new_ffn.append(ffn_new)
    return x, jnp.stack(new[0]), jnp.stack(new[2]), jnp.stack(new_ffn), new[1]


def _cast_kernel(*refs):
    n = len(refs) // 2
    for src, dst in zip(refs[:n], refs[n:]):
        dst[...] = src[...].astype(_BF16)


def _cast_layer0(big):
    n_steps = 8
    ins, in_specs, outs, out_specs = [], [], [], []
    for name, idx in _big_weights(0):
        stack = big[name]
        n_rows, width = stack.shape[1:]
        blk_rows = _cast_rows(n_rows, n_steps)
        last = n_rows // blk_rows - 1
        ins.append(stack)
        in_specs.append(pl.BlockSpec((None, blk_rows, width),
                                     lambda s, idx=idx, last=last: (idx, jnp.minimum(s, last), 0)))
        outs.append(jax.ShapeDtypeStruct((n_rows, width), _BF16))
        out_specs.append(pl.BlockSpec((blk_rows, width),
                                      lambda s, last=last: (jnp.minimum(s, last), 0)))
    res = pl.pallas_call(
        _cast_kernel, grid=(n_steps,), in_specs=in_specs, out_specs=out_specs, out_shape=outs,
        compiler_params=pltpu.CompilerParams(dimension_semantics=("arbitrary",),
                                             vmem_limit_bytes=VMEM_LIMIT_BYTES),
        name="cast_layer0",
    )(*ins)
    return {name: arr for (name, _), arr in zip(_big_weights(0), res)}


def _from_groups(a, group, k):
    *lead, n, c = a.shape
    a = a.reshape(*lead, n // (k * group), k, group, c)
    a = jnp.swapaxes(a, -3, -2)
    return a.reshape(*lead, n // k, k, c)


def kernel(x_prompt, x_sample, state_shortconv, state_pool, state_ffnconv, g_mix, g_ffn, g_final, a_w_in, a_conv, a_w_out, b_w_in, b_g_v, b_w_s, b_bias, b_w_out, c_w_group, c_scale, f_w_up, f_conv, f_conv_b, f_w_down):
    batch, seq, d = x_prompt.shape
    dec_batch, dec_seq, _ = x_sample.shape
    row = lambda a: a[:, None, :]
    w = dict(g_mix=row(g_mix), g_ffn=row(g_ffn), g_final=g_final[None, :],
             a_conv=a_conv, b_g_v=row(b_g_v), b_w_s=b_w_s, b_bias=b_bias,
             b_bias_t=jnp.swapaxes(b_bias, 1, 2), c_scale=row(c_scale),
             f_conv=f_conv, f_conv_b=row(f_conv_b))
    big = dict(a_w_in=a_w_in, a_w_out=a_w_out, b_w_in=b_w_in, b_w_out=b_w_out,
               c_w_group=c_w_group.reshape(c_w_group.shape[0], -1, c_w_group.shape[-1]),
               f_w_up=f_w_up, f_w_down=f_w_down)
    wbs = [_cast_layer0(big)]

    yp, conv_p, pool_p, ffn_p, _ = _trunk(
        x_prompt.reshape(batch * seq, d), None, None, None, w, wbs, big, prompt=True, group=batch)

    group = TILE_ROWS // dec_seq
    ys, conv_s, pool_s, ffn_s, v_s = _trunk(
        x_sample, state_shortconv, state_pool, state_ffnconv, w, wbs, big,
        prompt=False, group=group)
    fg = functools.partial(_from_groups, group=group)

    return (yp.reshape(batch, seq, d),
            fg(ys, k=dec_seq),
            conv_p, fg(conv_s, k=CONV_W - 1),
            pool_p, fg(pool_s, k=POOL_STATE),
            ffn_p, fg(ffn_s, k=CONV_W - 1),
            fg(jnp.stack(v_s), k=dec_seq))
```

```python
import functools

import jax
import jax.numpy as jnp
from jax import lax
from jax.experimental import pallas as pl
from jax.experimental.pallas import tpu as pltpu

EPS = 1e-6
CONV_W = 3
CHUNK = 128
POOL_WINDOWS = (2, 4, 8, 16)
POOL_STATE = max(POOL_WINDOWS) - 1

LANE_TILE = 256
SUBLANES = 8
PROMPT_ROWS = 1024
TILE_ROWS = 512
VMEM_LIMIT_BYTES = 60 * 1024 * 1024

_BF16 = jnp.bfloat16
_F32 = jnp.float32


def _rms(x, g):
    return x * lax.rsqrt(jnp.mean(x * x, axis=-1, keepdims=True) + EPS) * g


def _dot(a, b):
    return jnp.dot(a, b, preferred_element_type=_F32)


def _layer_spec(arr, layer):
    if layer is None:
        blk, idx = arr.shape, (0,) * arr.ndim
    else:
        blk, idx = (None,) + arr.shape[1:], (layer,) + (0,) * (arr.ndim - 1)
    return pl.BlockSpec(blk, lambda *_: idx, pipeline_mode=pl.Buffered(1))


def _big_weights(i):
    kind, j = i % 3, i // 3
    mixer = {0: ("a_w_in", "a_w_out"), 1: ("b_w_in", "b_w_out"), 2: ("c_w_group",)}[kind]
    return [(name, j) for name in mixer] + [("f_w_up", i), ("f_w_down", i)]


def _cast_rows(n_rows, n_steps):
    bf16_rows = 2 * SUBLANES
    if n_rows % (n_steps * bf16_rows) == 0:
        return n_rows // n_steps
    return CHUNK


def _delayed(cur, hist_ref, cols, k, *, step, prompt):
    rows = cur.shape[0]
    if prompt:
        rolled = pltpu.roll(cur, k, 0)
        row = lax.broadcasted_iota(jnp.int32, (SUBLANES, cur.shape[1]), 0)
        top = jnp.where(row < k, pltpu.roll(hist_ref[:, cols], k, 0), rolled[0:SUBLANES, :])
        return jnp.concatenate([top, rolled[SUBLANES:, :]], axis=0)
    past = [h[:, cols] for h in hist_ref[len(hist_ref) - k:]]
    return jnp.concatenate(past + [cur[0:rows - k * step, :]], axis=0)


def _conv3(cur, hist_ref, cw_ref, cols, *, step, prompt):
    return (cw_ref[0:1, cols] * _delayed(cur, hist_ref, cols, 2, step=step, prompt=prompt)
            + cw_ref[1:2, cols] * _delayed(cur, hist_ref, cols, 1, step=step, prompt=prompt)
            + cw_ref[2:3, cols] * cur)


def _save_tail(cur, hist_ref, st_ref, cols, *, step, prompt):
    rows = cur.shape[0]
    st_ref[:, cols] = cur[rows - (CONV_W - 1) * step:rows, :]
    if prompt:
        hist_ref[:, cols] = cur[rows - SUBLANES:rows, :]


def _sconv_part(x_ref, o_ref, hist_ref, st_ref, gn_ref, win_ref, cw_ref, wout_ref,
                h_s, z_s, *, step, prompt):
    d = x_ref.shape[1]
    h_s[...] = _rms(x_ref[...], gn_ref[...]).astype(_BF16)
    for c in range(d // LANE_TILE):
        cols = slice(c * LANE_TILE, (c + 1) * LANE_TILE)
        bg = _dot(h_s[...], win_ref[:, cols])
        cg = _dot(h_s[...], win_ref[:, d + c * LANE_TILE:d + (c + 1) * LANE_TILE])
        v = _dot(h_s[...], win_ref[:, 2 * d + c * LANE_TILE:2 * d + (c + 1) * LANE_TILE])
        u = cg * v
        y = _conv3(u, hist_ref, cw_ref, cols, step=step, prompt=prompt)
        z_s[:, cols] = (bg * y).astype(_BF16)
        _save_tail(u, hist_ref, st_ref, cols, step=step, prompt=prompt)
    o_ref[...] = x_ref[...] + _dot(z_s[...], wout_ref[...])


def _cmlp_prompt_part(x_ref, o_ref, gn_ref, win_ref, gv_ref, ws_ref, bias_ref, wout_ref,
                      h_s, z_s, vb_s, *, rows):
    d = x_ref.shape[1]
    n_heads = ws_ref.shape[0]
    head_dim = d // n_heads
    h_s[...] = _rms(x_ref[...], gn_ref[...]).astype(_BF16)
    vb_s[...] = _rms(_dot(h_s[...], win_ref[:, d:2 * d]), gv_ref[...]).astype(_BF16)

    t_idx = lax.broadcasted_iota(jnp.int32, (CHUNK, CHUNK), 0)
    s_idx = lax.broadcasted_iota(jnp.int32, (CHUNK, CHUNK), 1)
    causal = t_idx >= s_idx
    for hd in range(n_heads):
        cols = slice(hd * head_dim, (hd + 1) * head_dim)
        u = _dot(h_s[...], win_ref[:, cols])
        w_m = jnp.where(causal, ws_ref[hd], 0.0).astype(_BF16)
        bias = bias_ref[:, hd:hd + 1]
        for n in range(rows // CHUNK):
            rws = slice(n * CHUNK, (n + 1) * CHUNK)
            s = _dot(w_m, vb_s[rws, cols]) + bias
            z_s[rws, cols] = (u[rws, :] * s).astype(_BF16)
    o_ref[...] = x_ref[...] + _dot(z_s[...], wout_ref[...])


def _cmlp_sample_part(x_ref, o_ref, v_ref, wsm_ref, bsm_ref, gn_ref, win_ref, gv_ref, wout_ref,
                      h_s, z_s, *, step, rows):
    d = x_ref.shape[1]
    n_steps = rows // step
    n_heads = bsm_ref.shape[0] // n_steps
    head_dim = d // n_heads
    h_s[...] = _rms(x_ref[...], gn_ref[...]).astype(_BF16)
    v_ref[...] = _rms(_dot(h_s[...], win_ref[:, d:2 * d]), gv_ref[...])
    for hd in range(n_heads):
        cols = slice(hd * head_dim, (hd + 1) * head_dim)
        u = _dot(h_s[...], win_ref[:, cols])
        for t in range(n_steps):
            rws = slice(t * step, (t + 1) * step)
            s = jnp.full((step, head_dim), bsm_ref[hd * n_steps + t], _F32)
            for k in range(t + 1):
                w = wsm_ref[(hd * n_steps + t) * n_steps + k]
                s = s + w * v_ref[k * step:(k + 1) * step, cols]
            z_s[rws, cols] = (u[rws, :] * s).astype(_BF16)
    o_ref[...] = x_ref[...] + _dot(z_s[...], wout_ref[...])


def _pool_prompt_part(x_ref, o_ref, st_ref, gn_ref, wg_ref, sc_ref, hbuf, *, rows):
    hist = 2 * SUBLANES
    gdim = wg_ref.shape[1]
    j = pl.program_id(1)

    @pl.when(j == 0)
    def _():
        hbuf[0:hist, :] = jnp.zeros((hist, hbuf.shape[1]), _F32)

    @pl.when(j > 0)
    def _():
        hbuf[0:hist, :] = hbuf[rows:rows + hist, :]

    hbuf[hist:hist + rows, :] = _rms(x_ref[...], gn_ref[...])
    pos = j * rows + lax.broadcasted_iota(jnp.int32, (rows, 1), 0)
    for gi, w in enumerate(POOL_WINDOWS):
        cols = slice(gi * gdim, (gi + 1) * gdim)
        h = hbuf[hist:hist + rows, cols]
        tot = h
        for k in range(1, w):
            tot = tot + hbuf[hist - k:hist - k + rows, cols]
        cnt = jnp.minimum(pos + 1, w).astype(_F32)
        dlt = (tot / cnt - h).astype(_BF16)
        o_ref[:, cols] = x_ref[:, cols] + _dot(dlt, wg_ref[cols, :]) * sc_ref[:, cols]
    st_ref[...] = hbuf[hist + rows - POOL_STATE:hist + rows, :]


def _pool_sample_part(x_ref, o_ref, state_ref, st_ref, gn_ref, wg_ref, sc_ref, hbuf, *, step, rows):
    n_state = len(state_ref)
    n_new = rows // step
    gdim = wg_ref.shape[1]
    hbuf[...] = _rms(x_ref[...], gn_ref[...])
    for gi, w in enumerate(POOL_WINDOWS):
        cols = slice(gi * gdim, (gi + 1) * gdim)
        h = hbuf[:, cols]
        tot = h
        for k in range(1, w):
            past = [s[:, cols] for s in state_ref[n_state - k:n_state - k + n_new]]
            if k < n_new:
                past.append(hbuf[0:rows - k * step, cols])
            tot = tot + jnp.concatenate(past, axis=0)
        dlt = (tot / float(w) - h).astype(_BF16)
        o_ref[:, cols] = x_ref[:, cols] + _dot(dlt, wg_ref[cols, :]) * sc_ref[:, cols]
    for t in range(n_state - n_new):
        st_ref[t * step:(t + 1) * step, :] = state_ref[n_new + t][...]
    st_ref[(n_state - n_new) * step:n_state * step, :] = hbuf[...]


def _ffn_part(o_ref, hist_ref, st_ref, gn_ref, wup_ref, cw_ref, cb_ref, wdn_ref, gfin_ref,
              h_s, act_s, *, step, prompt):
    d_ff = wdn_ref.shape[0]
    h_s[...] = _rms(o_ref[...], gn_ref[...]).astype(_BF16)
    for c in range(d_ff // LANE_TILE):
        cols = slice(c * LANE_TILE, (c + 1) * LANE_TILE)
        g = _dot(h_s[...], wup_ref[:, cols])
        a = _dot(h_s[...], wup_ref[:, d_ff + c * LANE_TILE:d_ff + (c + 1) * LANE_TILE])
        y = _conv3(g, hist_ref, cw_ref, cols, step=step, prompt=prompt) + cb_ref[:, cols]
        act_s[:, cols] = (y * jax.nn.sigmoid(y) * a).astype(_BF16)
        _save_tail(g, hist_ref, st_ref, cols, step=step, prompt=prompt)
    out = o_ref[...] + _dot(act_s[...], wdn_ref[...])
    if gfin_ref is not None:
        out = _rms(out, gfin_ref[...])
    o_ref[...] = out


def _layer_kernel(*refs, kind, prompt, final_norm, step, rows, n_cast):
    it = iter(refs)
    nxt = lambda n=1: next(it) if n == 1 else [next(it) for _ in range(n)]
    conv_kw = dict(step=step, prompt=prompt)
    if kind == 1 and not prompt:
        wsm_ref, bsm_ref = nxt(2)
    x_ref = nxt()
    if not prompt:
        n_mix_state = {0: CONV_W - 1, 1: 0, 2: POOL_STATE}[kind]
        mix_state = [nxt() for _ in range(n_mix_state)]
        ffn_state = [nxt() for _ in range(CONV_W - 1)]
    n_mix_w = {0: 4, 1: 6 if prompt else 4, 2: 3}[kind]
    mix_w = nxt(n_mix_w)
    ffn_w = nxt(5)
    gfin_ref = nxt() if final_norm else None
    cast_in = [nxt() for _ in range(n_cast)]
    o_ref = nxt()
    mix_out = nxt() if not (prompt and kind == 1) else None
    ffn_st = nxt()
    cast_out = [nxt() for _ in range(n_cast)]
    h_s, act_s = nxt(2)

    for src, dst in zip(cast_in, cast_out):
        dst[...] = src[...].astype(_BF16)
    if prompt:
        ffn_carry = nxt()

        @pl.when(pl.program_id(1) == 0)
        def _():
            ffn_carry[...] = jnp.zeros(ffn_carry.shape, _F32)
    else:
        ffn_carry = ffn_state

    if kind == 0:
        z_s = nxt()
        if prompt:
            mix_carry = nxt()

            @pl.when(pl.program_id(1) == 0)
            def _():
                mix_carry[...] = jnp.zeros(mix_carry.shape, _F32)
        else:
            mix_carry = mix_state
        _sconv_part(x_ref, o_ref, mix_carry, mix_out, *mix_w, h_s, z_s, **conv_kw)
    elif kind == 1:
        if prompt:
            z_s, vb_s = nxt(2)
            _cmlp_prompt_part(x_ref, o_ref, *mix_w, h_s, z_s, vb_s, rows=rows)
        else:
            z_s = nxt()
            _cmlp_sample_part(x_ref, o_ref, mix_out, wsm_ref, bsm_ref, *mix_w, h_s, z_s,
                              step=step, rows=rows)
    else:
        hbuf = nxt()
        if prompt:
            _pool_prompt_part(x_ref, o_ref, mix_out, *mix_w, hbuf, rows=rows)
        else:
            _pool_sample_part(x_ref, o_ref, mix_state, mix_out, *mix_w, hbuf, step=step, rows=rows)

    _ffn_part(o_ref, ffn_carry, ffn_st, *ffn_w, gfin_ref, h_s, act_s, **conv_kw)


def _layer(x, i, mix_state, ffn_state, w, wb, cast_next, *, prompt, group):
    n, d = x.shape
    d_ff = wb["f_w_down"].shape[0]
    depth = w["g_mix"].shape[0]
    kind, j = i % 3, i // 3
    final_norm = i == depth - 1
    rows = PROMPT_ROWS if prompt else TILE_ROWS
    if prompt:
        n_t = n // group // rows
        grid = (group, n_t)
        step = 1
        x_spec = pl.BlockSpec((rows, d), lambda b, t: (b * n_t + t, 0))

        def state_spec(n_rows, width):
            return pl.BlockSpec((None, n_rows, width), lambda b, t: (b, 0, 0))

        def state_shape(n_rows, width):
            return jax.ShapeDtypeStruct((group, n_rows, width), _F32)
    else:
        grid = (n // rows,)
        step = group
        x_spec = pl.BlockSpec((rows, d), lambda g: (g, 0))

        def state_spec(n_rows, width):
            return pl.BlockSpec((n_rows * step, width), lambda g: (g, 0))

        def state_shape(n_rows, width):
            return jax.ShapeDtypeStruct((grid[0] * n_rows * step, width), _F32)

        def add_steps(arr, layer):
            n_k = arr.shape[1] // (grid[0] * group)
            for k in range(n_k):
                add_in(arr, pl.BlockSpec((None, group, arr.shape[2]),
                                         lambda g, k=k: (layer, g * n_k + k, 0)))

    ins, in_specs, outs, out_specs, scratch = [], [], [], [], []

    def add_in(arr, spec):
        ins.append(arr)
        in_specs.append(spec)

    def add_w(name, layer):
        if name in wb:
            add_in(wb[name], _layer_spec(wb[name], None))
        else:
            add_in(w[name], _layer_spec(w[name], layer))

    if kind == 1 and not prompt:
        n_steps = rows // step
        smem = pl.BlockSpec(memory_space=pltpu.SMEM)
        add_in(w["b_w_s"][j, :, :n_steps, :n_steps].reshape(-1), smem)
        add_in(w["b_bias"][j, :, :n_steps].reshape(-1), smem)
    add_in(x, x_spec)
    if not prompt:
        if kind != 1:
            add_steps(mix_state, j)
        add_steps(ffn_state, i)
    add_w("g_mix", i)
    if kind == 0:
        for name in ("a_w_in", "a_conv", "a_w_out"):
            add_w(name, j)
    elif kind == 1:
        names = ("b_w_in", "b_g_v", "b_w_s", "b_bias_t", "b_w_out") if prompt else (
            "b_w_in", "b_g_v", "b_w_out")
        for name in names:
            add_w(name, j)
    else:
        for name in ("c_w_group", "c_scale"):
            add_w(name, j)
    for name in ("g_ffn", "f_w_up", "f_conv", "f_conv_b", "f_w_down"):
        add_w(name, i)
    if final_norm:
        add_in(w["g_final"], pl.BlockSpec(w["g_final"].shape, lambda *_: (0, 0)))
    cast_specs = []
    for name, stack, idx in cast_next:
        n_rows, width = stack.shape[1:]
        blk_rows = _cast_rows(n_rows, n // rows)
        last = n_rows // blk_rows - 1
        add_in(stack, pl.BlockSpec(
            (None, blk_rows, width),
            lambda b, t, idx=idx, last=last: (idx, jnp.minimum(b * n_t + t, last), 0)))
        cast_specs.append(pl.BlockSpec(
            (blk_rows, width), lambda b, t, last=last: (jnp.minimum(b * n_t + t, last), 0)))

    outs.append(jax.ShapeDtypeStruct((n, d), _F32))
    out_specs.append(x_spec)
    if kind == 0:
        outs.append(state_shape(CONV_W - 1, d))
        out_specs.append(state_spec(CONV_W - 1, d))
    elif kind == 1 and not prompt:
        outs.append(jax.ShapeDtypeStruct((n, d), _F32))
        out_specs.append(x_spec)
    elif kind == 2:
        outs.append(state_shape(POOL_STATE, d))
        out_specs.append(state_spec(POOL_STATE, d))
    outs.append(state_shape(CONV_W - 1, d_ff))
    out_specs.append(state_spec(CONV_W - 1, d_ff))
    n_fixed_outs = len(outs)
    for (name, stack, idx), spec in zip(cast_next, cast_specs):
        outs.append(jax.ShapeDtypeStruct(stack.shape[1:], _BF16))
        out_specs.append(spec)

    scratch += [pltpu.VMEM((rows, d), _BF16), pltpu.VMEM((rows, d_ff), _BF16)]
    if prompt:
        scratch.append(pltpu.VMEM((SUBLANES, d_ff), _F32))
    if kind == 0:
        scratch.append(pltpu.VMEM((rows, d), _BF16))
        if prompt:
            scratch.append(pltpu.VMEM((SUBLANES, d), _F32))
    elif kind == 1:
        scratch.append(pltpu.VMEM((rows, d), _BF16))
        if prompt:
            scratch.append(pltpu.VMEM((rows, d), _BF16))
    else:
        scratch.append(pltpu.VMEM((2 * SUBLANES + rows if prompt else rows, d), _F32))

    res = pl.pallas_call(
        functools.partial(_layer_kernel, kind=kind, prompt=prompt, final_norm=final_norm,
                          step=step, rows=rows, n_cast=len(cast_next)),
        grid=grid,
        in_specs=in_specs,
        out_specs=out_specs,
        out_shape=outs,
        scratch_shapes=scratch,
        compiler_params=pltpu.CompilerParams(
            dimension_semantics=("arbitrary",) * len(grid),
            vmem_limit_bytes=VMEM_LIMIT_BYTES),
        name=f"layer{i}_{'prompt' if prompt else 'sample'}",
    )(*ins)
    x_new, ffn_new = res[0], res[n_fixed_outs - 1]
    mix_new = res[1] if n_fixed_outs == 3 else None
    wb_next = {name: arr for (name, _, _), arr in zip(cast_next, res[n_fixed_outs:])}
    return x_new, mix_new, ffn_new, wb_next


def _trunk(x, conv_prev, pool_prev, ffn_prev, w, wbs, big, *, prompt, group):
    depth = w["g_mix"].shape[0]
    new = {0: [], 1: [], 2: []}
    new_ffn = []
    for i in range(depth):
        kind = i % 3
        mix_state = conv_prev if kind == 0 else pool_prev
        cast_next = []
        if prompt and i + 1 < depth:
            cast_next = [(name, big[name], idx) for name, idx in _big_weights(i + 1)]
        x, mix_new, ffn_new, wb_next = _layer(x, i, mix_state, ffn_prev, w, wbs[i], cast_next,
                                              prompt=prompt, group=group)
        if cast_next:
            wbs.append(wb_next)
        if mix_new is not None:
            new[kind].append(mix_new)
        new_ffn.append(ffn_new)
    return x, jnp.stack(new[0]), jnp.stack(new[2]), jnp.stack(new_ffn), new[1]


def _cast_kernel(*refs):
    n = len(refs) // 2
    for src, dst in zip(refs[:n], refs[n:]):
        dst[...] = src[...].astype(_BF16)


def _cast_layer0(big):
    n_steps = 8
    ins, in_specs, outs, out_specs = [], [], [], []
    for name, idx in _big_weights(0):
        stack = big[name]
        n_rows, width = stack.shape[1:]
        blk_rows = _cast_rows(n_rows, n_steps)
        last = n_rows // blk_rows - 1
        ins.append(stack)
        in_specs.append(pl.BlockSpec((None, blk_rows, width),
                                     lambda s, idx=idx, last=last: (idx, jnp.minimum(s, last), 0)))
        outs.append(jax.ShapeDtypeStruct((n_rows, width), _BF16))
        out_specs.append(pl.BlockSpec((blk_rows, width),
                                      lambda s, last=last: (jnp.minimum(s, last), 0)))
    res = pl.pallas_call(
        _cast_kernel, grid=(n_steps,), in_specs=in_specs, out_specs=out_specs, out_shape=outs,
        compiler_params=pltpu.CompilerParams(dimension_semantics=("arbitrary",),
                                             vmem_limit_bytes=VMEM_LIMIT_BYTES),
        name="cast_layer0",
    )(*ins)
    return {name: arr for (name, _), arr in zip(_big_weights(0), res)}


def _to_groups(a, group):
    *lead, b, k, c = a.shape
    a = a.reshape(*lead, b // group, group, k, c)
    a = jnp.swapaxes(a, -3, -2)
    return a.reshape(*lead, b * k, c)


def _from_groups(a, group, k):
    *lead, n, c = a.shape
    a = a.reshape(*lead, n // (k * group), k, group, c)
    a = jnp.swapaxes(a, -3, -2)
    return a.reshape(*lead, n // k, k, c)


def kernel(x_prompt, x_sample, state_shortconv, state_pool, state_ffnconv, g_mix, g_ffn, g_final, a_w_in, a_conv, a_w_out, b_w_in, b_g_v, b_w_s, b_bias, b_w_out, c_w_group, c_scale, f_w_up, f_conv, f_conv_b, f_w_down):
    batch, seq, d = x_prompt.shape
    dec_batch, dec_seq, _ = x_sample.shape
    row = lambda a: a[:, None, :]
    w = dict(g_mix=row(g_mix), g_ffn=row(g_ffn), g_final=g_final[None, :],
             a_conv=a_conv, b_g_v=row(b_g_v), b_w_s=b_w_s, b_bias=b_bias,
             b_bias_t=jnp.swapaxes(b_bias, 1, 2), c_scale=row(c_scale),
             f_conv=f_conv, f_conv_b=row(f_conv_b))
    big = dict(a_w_in=a_w_in, a_w_out=a_w_out, b_w_in=b_w_in, b_w_out=b_w_out,
               c_w_group=c_w_group.reshape(c_w_group.shape[0], -1, c_w_group.shape[-1]),
               f_w_up=f_w_up, f_w_down=f_w_down)
    wbs = [_cast_layer0(big)]

    yp, conv_p, pool_p, ffn_p, _ = _trunk(
        x_prompt.reshape(batch * seq, d), None, None, None, w, wbs, big, prompt=True, group=batch)

    group = TILE_ROWS // dec_seq
    tg = functools.partial(_to_groups, group=group)
    ys, conv_s, pool_s, ffn_s, v_s = _trunk(
        tg(x_sample), tg(state_shortconv), tg(state_pool), tg(state_ffnconv), w, wbs, big,
        prompt=False, group=group)
    fg = functools.partial(_from_groups, group=group)

    return (yp.reshape(batch, seq, d),
            fg(ys, k=dec_seq),
            conv_p, fg(conv_s, k=CONV_W - 1),
            pool_p, fg(pool_s, k=POOL_STATE),
            ffn_p, fg(ffn_s, k=CONV_W - 1),
            fg(jnp.stack(v_s), k=dec_seq))
```

```python
import functools

import jax
import jax.numpy as jnp
from jax import lax
from jax.experimental import pallas as pl
from jax.experimental.pallas import tpu as pltpu

EPS = 1e-6
CONV_W = 3
CHUNK = 128
POOL_WINDOWS = (2, 4, 8, 16)
POOL_STATE = max(POOL_WINDOWS) - 1
assert all(w & (w - 1) == 0 for w in POOL_WINDOWS)

LANE_TILE = 256
SUBLANES = 8
PROMPT_ROWS = 1024
TILE_ROWS = 512
VMEM_LIMIT_BYTES = 60 * 1024 * 1024

_BF16 = jnp.bfloat16
_F32 = jnp.float32


def _rms(x, g):
    return x * lax.rsqrt(jnp.mean(x * x, axis=-1, keepdims=True) + EPS) * g


def _dot(a, b):
    return jnp.dot(a, b, preferred_element_type=_F32)


def _layer_spec(arr, layer):
    if layer is None:
        blk, idx = arr.shape, (0,) * arr.ndim
    else:
        blk, idx = (None,) + arr.shape[1:], (layer,) + (0,) * (arr.ndim - 1)
    return pl.BlockSpec(blk, lambda *_: idx, pipeline_mode=pl.Buffered(1))


def _big_weights(i):
    kind, j = i % 3, i // 3
    mixer = {0: ("a_w_in", "a_w_out"), 1: ("b_w_in", "b_w_out"), 2: ("c_w_group",)}[kind]
    return [(name, j) for name in mixer] + [("f_w_up", i), ("f_w_down", i)]


def _cast_rows(n_rows, n_steps):
    bf16_rows = 2 * SUBLANES
    if n_rows % (n_steps * bf16_rows) == 0:
        return n_rows // n_steps
    return CHUNK


def _delayed(cur, hist_ref, cols, k, *, step, prompt):
    rows = cur.shape[0]
    if prompt:
        rolled = pltpu.roll(cur, k, 0)
        row = lax.broadcasted_iota(jnp.int32, (SUBLANES, cur.shape[1]), 0)
        top = jnp.where(row < k, pltpu.roll(hist_ref[:, cols], k, 0), rolled[0:SUBLANES, :])
        return jnp.concatenate([top, rolled[SUBLANES:, :]], axis=0)
    past = [h[:, cols] for h in hist_ref[len(hist_ref) - k:]]
    return jnp.concatenate(past + [cur[0:rows - k * step, :]], axis=0)


def _conv3(cur, hist_ref, cw_ref, cols, *, step, prompt):
    return (cw_ref[0:1, cols] * _delayed(cur, hist_ref, cols, 2, step=step, prompt=prompt)
            + cw_ref[1:2, cols] * _delayed(cur, hist_ref, cols, 1, step=step, prompt=prompt)
            + cw_ref[2:3, cols] * cur)


def _save_tail(cur, hist_ref, st_ref, cols, *, step, prompt):
    rows = cur.shape[0]
    st_ref[:, cols] = cur[rows - (CONV_W - 1) * step:rows, :]
    if prompt:
        hist_ref[:, cols] = cur[rows - SUBLANES:rows, :]


def _sconv_part(x_ref, o_ref, hist_ref, st_ref, gn_ref, win_ref, cw_ref, wout_ref,
                h_s, z_s, *, step, prompt):
    d = x_ref.shape[1]
    h_s[...] = _rms(x_ref[...], gn_ref[...]).astype(_BF16)
    for c in range(d // LANE_TILE):
        cols = slice(c * LANE_TILE, (c + 1) * LANE_TILE)
        bg = _dot(h_s[...], win_ref[:, cols])
        cg = _dot(h_s[...], win_ref[:, d + c * LANE_TILE:d + (c + 1) * LANE_TILE])
        v = _dot(h_s[...], win_ref[:, 2 * d + c * LANE_TILE:2 * d + (c + 1) * LANE_TILE])
        u = cg * v
        y = _conv3(u, hist_ref, cw_ref, cols, step=step, prompt=prompt)
        z_s[:, cols] = (bg * y).astype(_BF16)
        _save_tail(u, hist_ref, st_ref, cols, step=step, prompt=prompt)
    o_ref[...] = x_ref[...] + _dot(z_s[...], wout_ref[...])


def _cmlp_prompt_part(x_ref, o_ref, gn_ref, win_ref, gv_ref, ws_ref, bias_ref, wout_ref,
                      h_s, z_s, vb_s, *, rows):
    d = x_ref.shape[1]
    n_heads = ws_ref.shape[0]
    head_dim = d // n_heads
    h_s[...] = _rms(x_ref[...], gn_ref[...]).astype(_BF16)
    vb_s[...] = _rms(_dot(h_s[...], win_ref[:, d:2 * d]), gv_ref[...]).astype(_BF16)

    t_idx = lax.broadcasted_iota(jnp.int32, (CHUNK, CHUNK), 0)
    s_idx = lax.broadcasted_iota(jnp.int32, (CHUNK, CHUNK), 1)
    causal = t_idx >= s_idx
    for hd in range(n_heads):
        cols = slice(hd * head_dim, (hd + 1) * head_dim)
        u = _dot(h_s[...], win_ref[:, cols])
        w_m = jnp.where(causal, ws_ref[hd], 0.0).astype(_BF16)
        bias = bias_ref[:, hd:hd + 1]
        for n in range(rows // CHUNK):
            rws = slice(n * CHUNK, (n + 1) * CHUNK)
            s = _dot(w_m, vb_s[rws, cols]) + bias
            z_s[rws, cols] = (u[rws, :] * s).astype(_BF16)
    o_ref[...] = x_ref[...] + _dot(z_s[...], wout_ref[...])


def _cmlp_sample_part(x_ref, o_ref, v_ref, wsm_ref, bsm_ref, gn_ref, win_ref, gv_ref, wout_ref,
                      h_s, z_s, *, step, rows):
    d = x_ref.shape[1]
    n_steps = rows // step
    n_heads = bsm_ref.shape[0] // n_steps
    head_dim = d // n_heads
    h_s[...] = _rms(x_ref[...], gn_ref[...]).astype(_BF16)
    v_ref[...] = _rms(_dot(h_s[...], win_ref[:, d:2 * d]), gv_ref[...])
    for hd in range(n_heads):
        cols = slice(hd * head_dim, (hd + 1) * head_dim)
        u = _dot(h_s[...], win_ref[:, cols])
        for t in range(n_steps):
            rws = slice(t * step, (t + 1) * step)
            s = jnp.full((step, head_dim), bsm_ref[hd * n_steps + t], _F32)
            for k in range(t + 1):
                w = wsm_ref[(hd * n_steps + t) * n_steps + k]
                s = s + w * v_ref[k * step:(k + 1) * step, cols]
            z_s[rws, cols] = (u[rws, :] * s).astype(_BF16)
    o_ref[...] = x_ref[...] + _dot(z_s[...], wout_ref[...])


def _pool_prompt_part(x_ref, o_ref, st_ref, gn_ref, wg_ref, sc_ref, hbuf, *, rows):
    hist = 2 * SUBLANES
    gdim = wg_ref.shape[1]
    j = pl.program_id(1)

    @pl.when(j == 0)
    def _():
        hbuf[0:hist, :] = jnp.zeros((hist, hbuf.shape[1]), _F32)

    @pl.when(j > 0)
    def _():
        hbuf[0:hist, :] = hbuf[rows:rows + hist, :]

    hbuf[hist:hist + rows, :] = _rms(x_ref[...], gn_ref[...])
    pos = j * rows + lax.broadcasted_iota(jnp.int32, (rows, 1), 0)
    for gi, w in enumerate(POOL_WINDOWS):
        cols = slice(gi * gdim, (gi + 1) * gdim)
        tot = hbuf[:, cols]
        span = 1
        while span < w:
            tot = tot + pltpu.roll(tot, span, 0)
            span *= 2
        tot = tot[hist:, :]
        h = hbuf[hist:hist + rows, cols]
        cnt = jnp.minimum(pos + 1, w).astype(_F32)
        dlt = (tot / cnt - h).astype(_BF16)
        o_ref[:, cols] = x_ref[:, cols] + _dot(dlt, wg_ref[cols, :]) * sc_ref[:, cols]
    st_ref[...] = hbuf[hist + rows - POOL_STATE:hist + rows, :]


def _pool_sample_part(x_ref, o_ref, state_ref, st_ref, gn_ref, wg_ref, sc_ref, hbuf, *, step, rows):
    n_state = len(state_ref)
    n_new = rows // step
    gdim = wg_ref.shape[1]
    hbuf[...] = _rms(x_ref[...], gn_ref[...])
    for gi, w in enumerate(POOL_WINDOWS):
        cols = slice(gi * gdim, (gi + 1) * gdim)
        h = hbuf[:, cols]
        tot = h
        for k in range(1, w):
            past = [s[:, cols] for s in state_ref[n_state - k:n_state - k + n_new]]
            if k < n_new:
                past.append(hbuf[0:rows - k * step, cols])
            tot = tot + jnp.concatenate(past, axis=0)
        dlt = (tot / float(w) - h).astype(_BF16)
        o_ref[:, cols] = x_ref[:, cols] + _dot(dlt, wg_ref[cols, :]) * sc_ref[:, cols]
    for t in range(n_state - n_new):
        st_ref[t * step:(t + 1) * step, :] = state_ref[n_new + t][...]
    st_ref[(n_state - n_new) * step:n_state * step, :] = hbuf[...]


def _ffn_part(o_ref, hist_ref, st_ref, gn_ref, wup_ref, cw_ref, cb_ref, wdn_ref, gfin_ref,
              h_s, act_s, *, step, prompt):
    d_ff = wdn_ref.shape[0]
    h_s[...] = _rms(o_ref[...], gn_ref[...]).astype(_BF16)
    for c in range(d_ff // LANE_TILE):
        cols = slice(c * LANE_TILE, (c + 1) * LANE_TILE)
        g = _dot(h_s[...], wup_ref[:, cols])
        a = _dot(h_s[...], wup_ref[:, d_ff + c * LANE_TILE:d_ff + (c + 1) * LANE_TILE])
        y = _conv3(g, hist_ref, cw_ref, cols, step=step, prompt=prompt) + cb_ref[:, cols]
        act_s[:, cols] = (y * jax.nn.sigmoid(y) * a).astype(_BF16)
        _save_tail(g, hist_ref, st_ref, cols, step=step, prompt=prompt)
    out = o_ref[...] + _dot(act_s[...], wdn_ref[...])
    if gfin_ref is not None:
        out = _rms(out, gfin_ref[...])
    o_ref[...] = out


def _layer_kernel(*refs, kind, prompt, final_norm, step, rows, n_cast):
    it = iter(refs)
    nxt = lambda n=1: next(it) if n == 1 else [next(it) for _ in range(n)]
    conv_kw = dict(step=step, prompt=prompt)
    if kind == 1 and not prompt:
        wsm_ref, bsm_ref = nxt(2)
    x_ref = nxt()
    if not prompt:
        n_mix_state = {0: CONV_W - 1, 1: 0, 2: POOL_STATE}[kind]
        mix_state = [nxt() for _ in range(n_mix_state)]
        ffn_state = [nxt() for _ in range(CONV_W - 1)]
    n_mix_w = {0: 4, 1: 6 if prompt else 4, 2: 3}[kind]
    mix_w = nxt(n_mix_w)
    ffn_w = nxt(5)
    gfin_ref = nxt() if final_norm else None
    cast_in = [nxt() for _ in range(n_cast)]
    o_ref = nxt()
    mix_out = nxt() if not (prompt and kind == 1) else None
    ffn_st = nxt()
    cast_out = [nxt() for _ in range(n_cast)]
    h_s, act_s = nxt(2)

    for src, dst in zip(cast_in, cast_out):
        dst[...] = src[...].astype(_BF16)
    if prompt:
        ffn_carry = nxt()

        @pl.when(pl.program_id(1) == 0)
        def _():
            ffn_carry[...] = jnp.zeros(ffn_carry.shape, _F32)
    else:
        ffn_carry = ffn_state

    if kind == 0:
        z_s = nxt()
        if prompt:
            mix_carry = nxt()

            @pl.when(pl.program_id(1) == 0)
            def _():
                mix_carry[...] = jnp.zeros(mix_carry.shape, _F32)
        else:
            mix_carry = mix_state
        _sconv_part(x_ref, o_ref, mix_carry, mix_out, *mix_w, h_s, z_s, **conv_kw)
    elif kind == 1:
        if prompt:
            z_s, vb_s = nxt(2)
            _cmlp_prompt_part(x_ref, o_ref, *mix_w, h_s, z_s, vb_s, rows=rows)
        else:
            z_s = nxt()
            _cmlp_sample_part(x_ref, o_ref, mix_out, wsm_ref, bsm_ref, *mix_w, h_s, z_s,
                              step=step, rows=rows)
    else:
        hbuf = nxt()
        if prompt:
            _pool_prompt_part(x_ref, o_ref, mix_out, *mix_w, hbuf, rows=rows)
        else:
            _pool_sample_part(x_ref, o_ref, mix_state, mix_out, *mix_w, hbuf, step=step, rows=rows)

    _ffn_part(o_ref, ffn_carry, ffn_st, *ffn_w, gfin_ref, h_s, act_s, **conv_kw)


def _layer(x, i, mix_state, ffn_state, w, wb, cast_next, *, prompt, group):
    n, d = x.shape
    d_ff = wb["f_w_down"].shape[0]
    depth = w["g_mix"].shape[0]
    kind, j = i % 3, i // 3
    final_norm = i == depth - 1
    rows = PROMPT_ROWS if prompt else TILE_ROWS
    if prompt:
        n_t = n // group // rows
        grid = (group, n_t)
        step = 1
        x_spec = pl.BlockSpec((rows, d), lambda b, t: (b * n_t + t, 0))

        def state_spec(n_rows, width):
            return pl.BlockSpec((None, n_rows, width), lambda b, t: (b, 0, 0))

        def state_shape(n_rows, width):
            return jax.ShapeDtypeStruct((group, n_rows, width), _F32)
    else:
        grid = (n // rows,)
        step = group
        x_spec = pl.BlockSpec((rows, d), lambda g: (g, 0))

        def state_spec(n_rows, width):
            return pl.BlockSpec((n_rows * step, width), lambda g: (g, 0))

        def state_shape(n_rows, width):
            return jax.ShapeDtypeStruct((grid[0] * n_rows * step, width), _F32)

        def add_steps(arr, layer):
            n_k = arr.shape[1] // (grid[0] * group)
            for k in range(n_k):
                add_in(arr, pl.BlockSpec((None, group, arr.shape[2]),
                                         lambda g, k=k: (layer, g * n_k + k, 0)))

    ins, in_specs, outs, out_specs, scratch = [], [], [], [], []

    def add_in(arr, spec):
        ins.append(arr)
        in_specs.append(spec)

    def add_w(name, layer):
        if name in wb:
            add_in(wb[name], _layer_spec(wb[name], None))
        else:
            add_in(w[name], _layer_spec(w[name], layer))

    if kind == 1 and not prompt:
        n_steps = rows // step
        smem = pl.BlockSpec(memory_space=pltpu.SMEM)
        add_in(w["b_w_s"][j, :, :n_steps, :n_steps].reshape(-1), smem)
        add_in(w["b_bias"][j, :, :n_steps].reshape(-1), smem)
    add_in(x, x_spec)
    if not prompt:
        if kind != 1:
            add_steps(mix_state, j)
        add_steps(ffn_state, i)
    add_w("g_mix", i)
    if kind == 0:
        for name in ("a_w_in", "a_conv", "a_w_out"):
            add_w(name, j)
    elif kind == 1:
        names = ("b_w_in", "b_g_v", "b_w_s", "b_bias_t", "b_w_out") if prompt else (
            "b_w_in", "b_g_v", "b_w_out")
        for name in names:
            add_w(name, j)
    else:
        for name in ("c_w_group", "c_scale"):
            add_w(name, j)
    for name in ("g_ffn", "f_w_up", "f_conv", "f_conv_b", "f_w_down"):
        add_w(name, i)
    if final_norm:
        add_in(w["g_final"], pl.BlockSpec(w["g_final"].shape, lambda *_: (0, 0)))
    cast_specs = []
    for name, stack, idx in cast_next:
        n_rows, width = stack.shape[1:]
        blk_rows = _cast_rows(n_rows, n // rows)
        last = n_rows // blk_rows - 1
        add_in(stack, pl.BlockSpec(
            (None, blk_rows, width),
            lambda b, t, idx=idx, last=last: (idx, jnp.minimum(b * n_t + t, last), 0)))
        cast_specs.append(pl.BlockSpec(
            (blk_rows, width), lambda b, t, last=last: (jnp.minimum(b * n_t + t, last), 0)))

    outs.append(jax.ShapeDtypeStruct((n, d), _F32))
    out_specs.append(x_spec)
    if kind == 0:
        outs.append(state_shape(CONV_W - 1, d))
        out_specs.append(state_spec(CONV_W - 1, d))
    elif kind == 1 and not prompt:
        outs.append(jax.ShapeDtypeStruct((n, d), _F32))
        out_specs.append(x_spec)
    elif kind == 2:
        outs.append(state_shape(POOL_STATE, d))
        out_specs.append(state_spec(POOL_STATE, d))
    outs.append(state_shape(CONV_W - 1, d_ff))
    out_specs.append(state_spec(CONV_W - 1, d_ff))
    n_fixed_outs = len(outs)
    for (name, stack, idx), spec in zip(cast_next, cast_specs):
        outs.append(jax.ShapeDtypeStruct(stack.shape[1:], _BF16))
        out_specs.append(spec)

    scratch += [pltpu.VMEM((rows, d), _BF16), pltpu.VMEM((rows, d_ff), _BF16)]
    if prompt:
        scratch.append(pltpu.VMEM((SUBLANES, d_ff), _F32))
    if kind == 0:
        scratch.append(pltpu.VMEM((rows, d), _BF16))
        if prompt:
            scratch.append(pltpu.VMEM((SUBLANES, d), _F32))
    elif kind == 1:
        scratch.append(pltpu.VMEM((rows, d), _BF16))
        if prompt:
            scratch.append(pltpu.VMEM((rows, d), _BF16))
    else:
        scratch.append(pltpu.VMEM((2 * SUBLANES + rows if prompt else rows, d), _F32))

    res = pl.pallas_call(
        functools.partial(_layer_kernel, kind=kind, prompt=prompt, final_norm=final_norm,
                          step=step, rows=rows, n_cast=len(cast_next)),
        grid=grid,
        in_specs=in_specs,
        out_specs=out_specs,
        out_shape=outs,
        scratch_shapes=scratch,
        compiler_params=pltpu.CompilerParams(
            dimension_semantics=("arbitrary",) * len(grid),
            vmem_limit_bytes=VMEM_LIMIT_BYTES),
        name=f"layer{i}_{'prompt' if prompt else 'sample'}",
    )(*ins)
    x_new, ffn_new = res[0], res[n_fixed_outs - 1]
    mix_new = res[1] if n_fixed_outs == 3 else None
    wb_next = {name: arr for (name, _, _), arr in zip(cast_next, res[n_fixed_outs:])}
    return x_new, mix_new, ffn_new, wb_next


def _trunk(x, conv_prev, pool_prev, ffn_prev, w, wbs, big, *, prompt, group):
    depth = w["g_mix"].shape[0]
    new = {0: [], 1: [], 2: []}
    new_ffn = []
    for i in range(depth):
        kind = i % 3
        mix_state = conv_prev if kind == 0 else pool_prev
        cast_next = []
        if prompt and i + 1 < depth:
            cast_next = [(name, big[name], idx) for name, idx in _big_weights(i + 1)]
        x, mix_new, ffn_new, wb_next = _layer(x, i, mix_state, ffn_prev, w, wbs[i], cast_next,
                                              prompt=prompt, group=group)
        if cast_next:
            wbs.append(wb_next)
        if mix_new is not None:
            new[kind].append(mix_new)
        new_ffn.append(ffn_new)
    return x, jnp.stack(new[0]), jnp.stack(new[2]), jnp.stack(new_ffn), new[1]


def _cast_kernel(*refs):
    n = len(refs) // 2
    for src, dst in zip(refs[:n], refs[n:]):
        dst[...] = src[...].astype(_BF16)


def _cast_layer0(big):
    n_steps = 8
    ins, in_specs, outs, out_specs = [], [], [], []
    for name, idx in _big_weights(0):
        stack = big[name]
        n_rows, width = stack.shape[1:]
        blk_rows = _cast_rows(n_rows, n_steps)
        last = n_rows // blk_rows - 1
        ins.append(stack)
        in_specs.append(pl.BlockSpec((None, blk_rows, width),
                                     lambda s, idx=idx, last=last: (idx, jnp.minimum(s, last), 0)))
        outs.append(jax.ShapeDtypeStruct((n_rows, width), _BF16))
        out_specs.append(pl.BlockSpec((blk_rows, width),
                                      lambda s, last=last: (jnp.minimum(s, last), 0)))
    res = pl.pallas_call(
        _cast_kernel, grid=(n_steps,), in_specs=in_specs, out_specs=out_specs, out_shape=outs,
        compiler_params=pltpu.CompilerParams(dimension_semantics=("arbitrary",),
                                             vmem_limit_bytes=VMEM_LIMIT_BYTES),
        name="cast_layer0",
    )(*ins)
    return {name: arr for (name, _), arr in zip(_big_weights(0), res)}


def _to_groups(a, group):
    *lead, b, k, c = a.shape
    a = a.reshape(*lead, b // group, group, k, c)
    a = jnp.swapaxes(a, -3, -2)
    return a.reshape(*lead, b * k, c)


def _from_groups(a, group, k):
    *lead, n, c = a.shape
    a = a.reshape(*lead, n // (k * group), k, group, c)
    a = jnp.swapaxes(a, -3, -2)
    return a.reshape(*lead, n // k, k, c)


def kernel(x_prompt, x_sample, state_shortconv, state_pool, state_ffnconv, g_mix, g_ffn, g_final, a_w_in, a_conv, a_w_out, b_w_in, b_g_v, b_w_s, b_bias, b_w_out, c_w_group, c_scale, f_w_up, f_conv, f_conv_b, f_w_down):
    batch, seq, d = x_prompt.shape
    dec_batch, dec_seq, _ = x_sample.shape
    row = lambda a: a[:, None, :]
    w = dict(g_mix=row(g_mix), g_ffn=row(g_ffn), g_final=g_final[None, :],
             a_conv=a_conv, b_g_v=row(b_g_v), b_w_s=b_w_s, b_bias=b_bias,
             b_bias_t=jnp.swapaxes(b_bias, 1, 2), c_scale=row(c_scale),
             f_conv=f_conv, f_conv_b=row(f_conv_b))
    big = dict(a_w_in=a_w_in, a_w_out=a_w_out, b_w_in=b_w_in, b_w_out=b_w_out,
               c_w_group=c_w_group.reshape(c_w_group.shape[0], -1, c_w_group.shape[-1]),
               f_w_up=f_w_up, f_w_down=f_w_down)
    wbs = [_cast_layer0(big)]

    yp, conv_p, pool_p, ffn_p, _ = _trunk(
        x_prompt.reshape(batch * seq, d), None, None, None, w, wbs, big, prompt=True, group=batch)

    group = TILE_ROWS // dec_seq
    tg = functools.partial(_to_groups, group=group)
    ys, conv_s, pool_s, ffn_s, v_s = _trunk(
        tg(x_sample), tg(state_shortconv), tg(state_pool), tg(state_ffnconv), w, wbs, big,
        prompt=False, group=group)
    fg = functools.partial(_from_groups, group=group)

    return (yp.reshape(batch, seq, d),
            fg(ys, k=dec_seq),
            conv_p, fg(conv_s, k=CONV_W - 1),
            pool_p, fg(pool_s, k=POOL_STATE),
            ffn_p, fg(ffn_s, k=CONV_W - 1),
            fg(jnp.stack(v_s), k=dec_seq))
```

```python
import functools

import jax
import jax.numpy as jnp
from jax import lax
from jax.experimental import pallas as pl
from jax.experimental.pallas import tpu as pltpu

EPS = 1e-6
CONV_W = 3
CHUNK = 128
POOL_WINDOWS = (2, 4, 8, 16)
POOL_STATE = max(POOL_WINDOWS) - 1
assert all(w & (w - 1) == 0 for w in POOL_WINDOWS)

LANE_TILE = 256
SUBLANES = 8
PROMPT_ROWS = 1024
TILE_ROWS = 512
VMEM_LIMIT_BYTES = 60 * 1024 * 1024

_BF16 = jnp.bfloat16
_F32 = jnp.float32


def _rms(x, g):
    return x * lax.rsqrt(jnp.mean(x * x, axis=-1, keepdims=True) + EPS) * g


def _dot(a, b):
    return jnp.dot(a, b, preferred_element_type=_F32)


def _layer_spec(arr, layer):
    if layer is None:
        blk, idx = arr.shape, (0,) * arr.ndim
    else:
        blk, idx = (None,) + arr.shape[1:], (layer,) + (0,) * (arr.ndim - 1)
    return pl.BlockSpec(blk, lambda *_: idx, pipeline_mode=pl.Buffered(1))


def _big_weights(i):
    kind, j = i % 3, i // 3
    mixer = {0: ("a_w_in", "a_w_out"), 1: ("b_w_in", "b_w_out"), 2: ("c_w_group",)}[kind]
    return [(name, j) for name in mixer] + [("f_w_up", i), ("f_w_down", i)]


def _cast_rows(n_rows, n_steps):
    bf16_rows = 2 * SUBLANES
    if n_rows % (n_steps * bf16_rows) == 0:
        return n_rows // n_steps
    return CHUNK


def _delayed(cur, hist_ref, cols, k, *, step, prompt):
    rows = cur.shape[0]
    if prompt:
        rolled = pltpu.roll(cur, k, 0)
        row = lax.broadcasted_iota(jnp.int32, (SUBLANES, cur.shape[1]), 0)
        top = jnp.where(row < k, pltpu.roll(hist_ref[:, cols], k, 0), rolled[0:SUBLANES, :])
        return jnp.concatenate([top, rolled[SUBLANES:, :]], axis=0)
    past = [h[:, cols] for h in hist_ref[len(hist_ref) - k:]]
    return jnp.concatenate(past + [cur[0:rows - k * step, :]], axis=0)


def _conv3(cur, hist_ref, cw_ref, cols, *, step, prompt):
    return (cw_ref[0:1, cols] * _delayed(cur, hist_ref, cols, 2, step=step, prompt=prompt)
            + cw_ref[1:2, cols] * _delayed(cur, hist_ref, cols, 1, step=step, prompt=prompt)
            + cw_ref[2:3, cols] * cur)


def _save_tail(cur, hist_ref, st_ref, cols, *, step, prompt):
    rows = cur.shape[0]
    st_ref[:, cols] = cur[rows - (CONV_W - 1) * step:rows, :]
    if prompt:
        hist_ref[:, cols] = cur[rows - SUBLANES:rows, :]


def _sconv_part(x_ref, o_ref, hist_ref, st_ref, gn_ref, win_ref, cw_ref, wout_ref,
                h_s, z_s, *, step, prompt):
    d = x_ref.shape[1]
    h_s[...] = _rms(x_ref[...], gn_ref[...]).astype(_BF16)
    for c in range(d // LANE_TILE):
        cols = slice(c * LANE_TILE, (c + 1) * LANE_TILE)
        bg = _dot(h_s[...], win_ref[:, cols])
        cg = _dot(h_s[...], win_ref[:, d + c * LANE_TILE:d + (c + 1) * LANE_TILE])
        v = _dot(h_s[...], win_ref[:, 2 * d + c * LANE_TILE:2 * d + (c + 1) * LANE_TILE])
        u = cg * v
        y = _conv3(u, hist_ref, cw_ref, cols, step=step, prompt=prompt)
        z_s[:, cols] = (bg * y).astype(_BF16)
        _save_tail(u, hist_ref, st_ref, cols, step=step, prompt=prompt)
    o_ref[...] = x_ref[...] + _dot(z_s[...], wout_ref[...])


def _cmlp_prompt_part(x_ref, o_ref, gn_ref, win_ref, gv_ref, ws_ref, bias_ref, wout_ref,
                      h_s, z_s, vb_s, *, rows):
    d = x_ref.shape[1]
    n_heads = ws_ref.shape[0]
    head_dim = d // n_heads
    h_s[...] = _rms(x_ref[...], gn_ref[...]).astype(_BF16)
    vb_s[...] = _rms(_dot(h_s[...], win_ref[:, d:2 * d]), gv_ref[...]).astype(_BF16)

    t_idx = lax.broadcasted_iota(jnp.int32, (CHUNK, CHUNK), 0)
    s_idx = lax.broadcasted_iota(jnp.int32, (CHUNK, CHUNK), 1)
    causal = t_idx >= s_idx
    for hd in range(n_heads):
        cols = slice(hd * head_dim, (hd + 1) * head_dim)
        u = _dot(h_s[...], win_ref[:, cols])
        w_m = jnp.where(causal, ws_ref[hd], 0.0).astype(_BF16)
        bias = bias_ref[:, hd:hd + 1]
        for n in range(rows // CHUNK):
            rws = slice(n * CHUNK, (n + 1) * CHUNK)
            s = _dot(w_m, vb_s[rws, cols]) + bias
            z_s[rws, cols] = (u[rws, :] * s).astype(_BF16)
    o_ref[...] = x_ref[...] + _dot(z_s[...], wout_ref[...])


def _cmlp_sample_part(x_ref, o_ref, v_ref, wsm_ref, bsm_ref, gn_ref, win_ref, gv_ref, wout_ref,
                      h_s, z_s, *, step, rows):
    d = x_ref.shape[1]
    n_steps = rows // step
    n_heads = bsm_ref.shape[0] // n_steps
    head_dim = d // n_heads
    h_s[...] = _rms(x_ref[...], gn_ref[...]).astype(_BF16)
    v_ref[...] = _rms(_dot(h_s[...], win_ref[:, d:2 * d]), gv_ref[...])
    for hd in range(n_heads):
        cols = slice(hd * head_dim, (hd + 1) * head_dim)
        u = _dot(h_s[...], win_ref[:, cols])
        for t in range(n_steps):
            rws = slice(t * step, (t + 1) * step)
            s = jnp.full((step, head_dim), bsm_ref[hd * n_steps + t], _F32)
            for k in range(t + 1):
                w = wsm_ref[(hd * n_steps + t) * n_steps + k]
                s = s + w * v_ref[k * step:(k + 1) * step, cols]
            z_s[rws, cols] = (u[rws, :] * s).astype(_BF16)
    o_ref[...] = x_ref[...] + _dot(z_s[...], wout_ref[...])


def _pool_prompt_part(x_ref, o_ref, st_ref, gn_ref, wg_ref, sc_ref, hbuf, *, rows):
    hist = 2 * SUBLANES
    gdim = wg_ref.shape[1]
    j = pl.program_id(1)

    @pl.when(j == 0)
    def _():
        hbuf[0:hist, :] = jnp.zeros((hist, hbuf.shape[1]), _F32)

    @pl.when(j > 0)
    def _():
        hbuf[0:hist, :] = hbuf[rows:rows + hist, :]

    hbuf[hist:hist + rows, :] = _rms(x_ref[...], gn_ref[...])
    pos = j * rows + lax.broadcasted_iota(jnp.int32, (rows, 1), 0)
    for gi, w in enumerate(POOL_WINDOWS):
        cols = slice(gi * gdim, (gi + 1) * gdim)
        tot = hbuf[:, cols]
        span = 1
        while span < w:
            tot = tot + pltpu.roll(tot, span, 0)
            span *= 2
        tot = tot[hist:, :]
        h = hbuf[hist:hist + rows, cols]
        cnt = jnp.minimum(pos + 1, w).astype(_F32)
        dlt = (tot / cnt - h).astype(_BF16)
        o_ref[:, cols] = x_ref[:, cols] + _dot(dlt, wg_ref[cols, :]) * sc_ref[:, cols]
    st_ref[...] = hbuf[hist + rows - POOL_STATE:hist + rows, :]


def _pool_sample_part(x_ref, o_ref, state_ref, st_ref, gn_ref, wg_ref, sc_ref, hbuf, *, step, rows):
    n_state = len(state_ref)
    n_new = rows // step
    gdim = wg_ref.shape[1]
    hbuf[...] = _rms(x_ref[...], gn_ref[...])
    for gi, w in enumerate(POOL_WINDOWS):
        cols = slice(gi * gdim, (gi + 1) * gdim)
        h = hbuf[:, cols]
        tot = h
        for k in range(1, w):
            past = [s[:, cols] for s in state_ref[n_state - k:n_state - k + n_new]]
            if k < n_new:
                past.append(hbuf[0:rows - k * step, cols])
            tot = tot + jnp.concatenate(past, axis=0)
        dlt = (tot / float(w) - h).astype(_BF16)
        o_ref[:, cols] = x_ref[:, cols] + _dot(dlt, wg_ref[cols, :]) * sc_ref[:, cols]
    for t in range(n_state - n_new):
        st_ref[t * step:(t + 1) * step, :] = state_ref[n_new + t][...]
    st_ref[(n_state - n_new) * step:n_state * step, :] = hbuf[...]


def _ffn_part(o_ref, hist_ref, st_ref, gn_ref, wup_ref, cw_ref, cb_ref, wdn_ref, gfin_ref,
              h_s, act_s, *, step, prompt):
    d_ff = wdn_ref.shape[0]
    h_s[...] = _rms(o_ref[...], gn_ref[...]).astype(_BF16)
    for c in range(d_ff // LANE_TILE):
        cols = slice(c * LANE_TILE, (c + 1) * LANE_TILE)
        g = _dot(h_s[...], wup_ref[:, cols])
        a = _dot(h_s[...], wup_ref[:, d_ff + c * LANE_TILE:d_ff + (c + 1) * LANE_TILE])
        y = _conv3(g, hist_ref, cw_ref, cols, step=step, prompt=prompt) + cb_ref[:, cols]
        act_s[:, cols] = (y * jax.nn.sigmoid(y) * a).astype(_BF16)
        _save_tail(g, hist_ref, st_ref, cols, step=step, prompt=prompt)
    out = o_ref[...] + _dot(act_s[...], wdn_ref[...])
    if gfin_ref is not None:
        out = _rms(out, gfin_ref[...])
    o_ref[...] = out


def _layer_kernel(*refs, kind, prompt, final_norm, step, rows, n_cast, n_grid):
    it = iter(refs)
    nxt = lambda n=1: next(it) if n == 1 else [next(it) for _ in range(n)]
    conv_kw = dict(step=step, prompt=prompt)
    if kind == 1 and not prompt:
        wsm_ref, bsm_ref = nxt(2)
    x_ref = nxt()
    if not prompt:
        n_mix_state = {0: CONV_W - 1, 1: 0, 2: POOL_STATE}[kind]
        mix_state = [nxt() for _ in range(n_mix_state)]
        ffn_state = [nxt() for _ in range(CONV_W - 1)]
    n_mix_w = {0: 4, 1: 6 if prompt else 4, 2: 3}[kind]
    mix_w = nxt(n_mix_w)
    gffn_ref, wup_hbm, fcw_ref, fcb_ref, wdn_hbm = nxt(5)
    gfin_ref = nxt() if final_norm else None
    cast_in = [nxt() for _ in range(n_cast)]
    o_ref = nxt()
    mix_out = nxt() if not (prompt and kind == 1) else None
    ffn_st = nxt()
    cast_out = [nxt() for _ in range(n_cast)]
    h_s, act_s, wup_v, wdn_v, w_sem = nxt(5)

    ffn_w = (gffn_ref, wup_v, fcw_ref, fcb_ref, wdn_v)
    ffn_copies = (pltpu.make_async_copy(wup_hbm, wup_v, w_sem.at[0]),
                  pltpu.make_async_copy(wdn_hbm, wdn_v, w_sem.at[1]))
    first_step = functools.reduce(jnp.logical_and, [pl.program_id(a) == 0 for a in range(n_grid)])

    @pl.when(first_step)
    def _():
        for cp in ffn_copies:
            cp.start()

    for src, dst in zip(cast_in, cast_out):
        dst[...] = src[...].astype(_BF16)
    if prompt:
        ffn_carry = nxt()

        @pl.when(pl.program_id(1) == 0)
        def _():
            ffn_carry[...] = jnp.zeros(ffn_carry.shape, _F32)
    else:
        ffn_carry = ffn_state

    if kind == 0:
        z_s = nxt()
        if prompt:
            mix_carry = nxt()

            @pl.when(pl.program_id(1) == 0)
            def _():
                mix_carry[...] = jnp.zeros(mix_carry.shape, _F32)
        else:
            mix_carry = mix_state
        _sconv_part(x_ref, o_ref, mix_carry, mix_out, *mix_w, h_s, z_s, **conv_kw)
    elif kind == 1:
        if prompt:
            z_s, vb_s = nxt(2)
            _cmlp_prompt_part(x_ref, o_ref, *mix_w, h_s, z_s, vb_s, rows=rows)
        else:
            z_s = nxt()
            _cmlp_sample_part(x_ref, o_ref, mix_out, wsm_ref, bsm_ref, *mix_w, h_s, z_s,
                              step=step, rows=rows)
    else:
        hbuf = nxt()
        if prompt:
            _pool_prompt_part(x_ref, o_ref, mix_out, *mix_w, hbuf, rows=rows)
        else:
            _pool_sample_part(x_ref, o_ref, mix_state, mix_out, *mix_w, hbuf, step=step, rows=rows)

    @pl.when(first_step)
    def _():
        for cp in ffn_copies:
            cp.wait()

    _ffn_part(o_ref, ffn_carry, ffn_st, *ffn_w, gfin_ref, h_s, act_s, **conv_kw)


def _layer(x, i, mix_state, ffn_state, w, wb, cast_next, *, prompt, group):
    n, d = x.shape
    d_ff = wb["f_w_down"].shape[0]
    depth = w["g_mix"].shape[0]
    kind, j = i % 3, i // 3
    final_norm = i == depth - 1
    rows = PROMPT_ROWS if prompt else TILE_ROWS
    if prompt:
        n_t = n // group // rows
        grid = (group, n_t)
        step = 1
        x_spec = pl.BlockSpec((rows, d), lambda b, t: (b * n_t + t, 0))

        def state_spec(n_rows, width):
            return pl.BlockSpec((None, n_rows, width), lambda b, t: (b, 0, 0))

        def state_shape(n_rows, width):
            return jax.ShapeDtypeStruct((group, n_rows, width), _F32)
    else:
        grid = (n // rows,)
        step = group
        x_spec = pl.BlockSpec((rows, d), lambda g: (g, 0))

        def state_spec(n_rows, width):
            return pl.BlockSpec((n_rows * step, width), lambda g: (g, 0))

        def state_shape(n_rows, width):
            return jax.ShapeDtypeStruct((grid[0] * n_rows * step, width), _F32)

        def add_steps(arr, layer):
            n_k = arr.shape[1] // (grid[0] * group)
            for k in range(n_k):
                add_in(arr, pl.BlockSpec((None, group, arr.shape[2]),
                                         lambda g, k=k: (layer, g * n_k + k, 0)))

    ins, in_specs, outs, out_specs, scratch = [], [], [], [], []

    def add_in(arr, spec):
        ins.append(arr)
        in_specs.append(spec)

    def add_w(name, layer):
        if name in wb:
            add_in(wb[name], _layer_spec(wb[name], None))
        else:
            add_in(w[name], _layer_spec(w[name], layer))

    if kind == 1 and not prompt:
        n_steps = rows // step
        smem = pl.BlockSpec(memory_space=pltpu.SMEM)
        add_in(w["b_w_s"][j, :, :n_steps, :n_steps].reshape(-1), smem)
        add_in(w["b_bias"][j, :, :n_steps].reshape(-1), smem)
    add_in(x, x_spec)
    if not prompt:
        if kind != 1:
            add_steps(mix_state, j)
        add_steps(ffn_state, i)
    add_w("g_mix", i)
    if kind == 0:
        for name in ("a_w_in", "a_conv", "a_w_out"):
            add_w(name, j)
    elif kind == 1:
        names = ("b_w_in", "b_g_v", "b_w_s", "b_bias_t", "b_w_out") if prompt else (
            "b_w_in", "b_g_v", "b_w_out")
        for name in names:
            add_w(name, j)
    else:
        for name in ("c_w_group", "c_scale"):
            add_w(name, j)
    for name in ("g_ffn", "f_w_up", "f_conv", "f_conv_b", "f_w_down"):
        if name in wb:
            add_in(wb[name], pl.BlockSpec(memory_space=pl.ANY))
        else:
            add_w(name, i)
    if final_norm:
        add_in(w["g_final"], pl.BlockSpec(w["g_final"].shape, lambda *_: (0, 0)))
    cast_specs = []
    for name, stack, idx in cast_next:
        n_rows, width = stack.shape[1:]
        blk_rows = _cast_rows(n_rows, n // rows)
        last = n_rows // blk_rows - 1
        add_in(stack, pl.BlockSpec(
            (None, blk_rows, width),
            lambda b, t, idx=idx, last=last: (idx, jnp.minimum(b * n_t + t, last), 0)))
        cast_specs.append(pl.BlockSpec(
            (blk_rows, width), lambda b, t, last=last: (jnp.minimum(b * n_t + t, last), 0)))

    outs.append(jax.ShapeDtypeStruct((n, d), _F32))
    out_specs.append(x_spec)
    if kind == 0:
        outs.append(state_shape(CONV_W - 1, d))
        out_specs.append(state_spec(CONV_W - 1, d))
    elif kind == 1 and not prompt:
        outs.append(jax.ShapeDtypeStruct((n, d), _F32))
        out_specs.append(x_spec)
    elif kind == 2:
        outs.append(state_shape(POOL_STATE, d))
        out_specs.append(state_spec(POOL_STATE, d))
    outs.append(state_shape(CONV_W - 1, d_ff))
    out_specs.append(state_spec(CONV_W - 1, d_ff))
    n_fixed_outs = len(outs)
    for (name, stack, idx), spec in zip(cast_next, cast_specs):
        outs.append(jax.ShapeDtypeStruct(stack.shape[1:], _BF16))
        out_specs.append(spec)

    scratch += [pltpu.VMEM((rows, d), _BF16), pltpu.VMEM((rows, d_ff), _BF16),
                pltpu.VMEM(wb["f_w_up"].shape, _BF16), pltpu.VMEM(wb["f_w_down"].shape, _BF16),
                pltpu.SemaphoreType.DMA((2,))]
    if prompt:
        scratch.append(pltpu.VMEM((SUBLANES, d_ff), _F32))
    if kind == 0:
        scratch.append(pltpu.VMEM((rows, d), _BF16))
        if prompt:
            scratch.append(pltpu.VMEM((SUBLANES, d), _F32))
    elif kind == 1:
        scratch.append(pltpu.VMEM((rows, d), _BF16))
        if prompt:
            scratch.append(pltpu.VMEM((rows, d), _BF16))
    else:
        scratch.append(pltpu.VMEM((2 * SUBLANES + rows if prompt else rows, d), _F32))

    res = pl.pallas_call(
        functools.partial(_layer_kernel, kind=kind, prompt=prompt, final_norm=final_norm,
                          step=step, rows=rows, n_cast=len(cast_next), n_grid=len(grid)),
        grid=grid,
        in_specs=in_specs,
        out_specs=out_specs,
        out_shape=outs,
        scratch_shapes=scratch,
        compiler_params=pltpu.CompilerParams(
            dimension_semantics=("arbitrary",) * len(grid),
            vmem_limit_bytes=VMEM_LIMIT_BYTES),
        name=f"layer{i}_{'prompt' if prompt else 'sample'}",
    )(*ins)
    x_new, ffn_new = res[0], res[n_fixed_outs - 1]
    mix_new = res[1] if n_fixed_outs == 3 else None
    wb_next = {name: arr for (name, _, _), arr in zip(cast_next, res[n_fixed_outs:])}
    return x_new, mix_new, ffn_new, wb_next


def _trunk(x, conv_prev, pool_prev, ffn_prev, w, wbs, big, *, prompt, group):
    depth = w["g_mix"].shape[0]
    new = {0: [], 1: [], 2: []}
    new_ffn = []
    for i in range(depth):
        kind = i % 3
        mix_state = conv_prev if kind == 0 else pool_prev
        cast_next = []
        if prompt and i + 1 < depth:
            cast_next = [(name, big[name], idx) for name, idx in _big_weights(i + 1)]
        x, mix_new, ffn_new, wb_next = _layer(x, i, mix_state, ffn_prev, w, wbs[i], cast_next,
                                              prompt=prompt, group=group)
        if cast_next:
            wbs.append(wb_next)
        if mix_new is not None:
            new[kind].append(mix_new)
        new_ffn.append(ffn_new)
    return x, jnp.stack(new[0]), jnp.stack(new[2]), jnp.stack(new_ffn), new[1]


def _cast_kernel(*refs):
    n = len(refs) // 2
    for src, dst in zip(refs[:n], refs[n:]):
        dst[...] = src[...].astype(_BF16)


def _cast_layer0(big):
    n_steps = 8
    ins, in_specs, outs, out_specs = [], [], [], []
    for name, idx in _big_weights(0):
        stack = big[name]
        n_rows, width = stack.shape[1:]
        blk_rows = _cast_rows(n_rows, n_steps)
        last = n_rows // blk_rows - 1
        ins.append(stack)
        in_specs.append(pl.BlockSpec((None, blk_rows, width),
                                     lambda s, idx=idx, last=last: (idx, jnp.minimum(s, last), 0)))
        outs.append(jax.ShapeDtypeStruct((n_rows, width), _BF16))
        out_specs.append(pl.BlockSpec((blk_rows, width),
                                      lambda s, last=last: (jnp.minimum(s, last), 0)))
    res = pl.pallas_call(
        _cast_kernel, grid=(n_steps,), in_specs=in_specs, out_specs=out_specs, out_shape=outs,
        compiler_params=pltpu.CompilerParams(dimension_semantics=("arbitrary",),
                                             vmem_limit_bytes=VMEM_LIMIT_BYTES),
        name="cast_layer0",
    )(*ins)
    return {name: arr for (name, _), arr in zip(_big_weights(0), res)}


def _to_groups(a, group):
    *lead, b, k, c = a.shape
    a = a.reshape(*lead, b // group, group, k, c)
    a = jnp.swapaxes(a, -3, -2)
    return a.reshape(*lead, b * k, c)


def _from_groups(a, group, k):
    *lead, n, c = a.shape
    a = a.reshape(*lead, n // (k * group), k, group, c)
    a = jnp.swapaxes(a, -3, -2)
    return a.reshape(*lead, n // k, k, c)


def kernel(x_prompt, x_sample, state_shortconv, state_pool, state_ffnconv, g_mix, g_ffn, g_final, a_w_in, a_conv, a_w_out, b_w_in, b_g_v, b_w_s, b_bias, b_w_out, c_w_group, c_scale, f_w_up, f_conv, f_conv_b, f_w_down):
    batch, seq, d = x_prompt.shape
    dec_batch, dec_seq, _ = x_sample.shape
    row = lambda a: a[:, None, :]
    w = dict(g_mix=row(g_mix), g_ffn=row(g_ffn), g_final=g_final[None, :],
             a_conv=a_conv, b_g_v=row(b_g_v), b_w_s=b_w_s, b_bias=b_bias,
             b_bias_t=jnp.swapaxes(b_bias, 1, 2), c_scale=row(c_scale),
             f_conv=f_conv, f_conv_b=row(f_conv_b))
    big = dict(a_w_in=a_w_in, a_w_out=a_w_out, b_w_in=b_w_in, b_w_out=b_w_out,
               c_w_group=c_w_group.reshape(c_w_group.shape[0], -1, c_w_group.shape[-1]),
               f_w_up=f_w_up, f_w_down=f_w_down)
    wbs = [_cast_layer0(big)]

    yp, conv_p, pool_p, ffn_p, _ = _trunk(
        x_prompt.reshape(batch * seq, d), None, None, None, w, wbs, big, prompt=True, group=batch)

    group = TILE_ROWS // dec_seq
    tg = functools.partial(_to_groups, group=group)
    ys, conv_s, pool_s, ffn_s, v_s = _trunk(
        tg(x_sample), tg(state_shortconv), tg(state_pool), tg(state_ffnconv), w, wbs, big,
        prompt=False, group=group)
    fg = functools.partial(_from_groups, group=group)

    return (yp.reshape(batch, seq, d),
            fg(ys, k=dec_seq),
            conv_p, fg(conv_s, k=CONV_W - 1),
            pool_p, fg(pool_s, k=POOL_STATE),
            ffn_p, fg(ffn_s, k=CONV_W - 1),
            fg(jnp.stack(v_s), k=dec_seq))
```

```python
import functools

import jax
import jax.numpy as jnp
from jax import lax
from jax.experimental import pallas as pl
from jax.experimental.pallas import tpu as pltpu

EPS = 1e-6
CONV_W = 3
CHUNK = 128
POOL_WINDOWS = (2, 4, 8, 16)
POOL_STATE = max(POOL_WINDOWS) - 1
assert all(w & (w - 1) == 0 for w in POOL_WINDOWS)

LANE_TILE = 256
SUBLANES = 8
PROMPT_ROWS = 1024
TILE_ROWS = 512
VMEM_LIMIT_BYTES = 60 * 1024 * 1024

_BF16 = jnp.bfloat16
_F32 = jnp.float32


def _rms(x, g):
    return x * lax.rsqrt(jnp.mean(x * x, axis=-1, keepdims=True) + EPS) * g


def _dot(a, b):
    return jnp.dot(a, b, preferred_element_type=_F32)


def _layer_spec(arr, layer):
    if layer is None:
        blk, idx = arr.shape, (0,) * arr.ndim
    else:
        blk, idx = (None,) + arr.shape[1:], (layer,) + (0,) * (arr.ndim - 1)
    return pl.BlockSpec(blk, lambda *_: idx, pipeline_mode=pl.Buffered(1))


def _big_weights(i):
    kind, j = i % 3, i // 3
    mixer = {0: ("a_w_in", "a_w_out"), 1: ("b_w_in", "b_w_out"), 2: ("c_w_group",)}[kind]
    return [(name, j) for name in mixer] + [("f_w_up", i), ("f_w_down", i)]


def _cast_rows(n_rows, n_steps):
    bf16_rows = 2 * SUBLANES
    if n_rows % (n_steps * bf16_rows) == 0:
        return n_rows // n_steps
    return CHUNK


def _delayed(cur, hist_ref, cols, k, *, step, prompt):
    rows = cur.shape[0]
    if prompt:
        rolled = pltpu.roll(cur, k, 0)
        row = lax.broadcasted_iota(jnp.int32, (SUBLANES, cur.shape[1]), 0)
        top = jnp.where(row < k, pltpu.roll(hist_ref[:, cols], k, 0), rolled[0:SUBLANES, :])
        return jnp.concatenate([top, rolled[SUBLANES:, :]], axis=0)
    past = [h[:, cols] for h in hist_ref[len(hist_ref) - k:]]
    return jnp.concatenate(past + [cur[0:rows - k * step, :]], axis=0)


def _conv3(cur, hist_ref, cw_ref, cols, *, step, prompt):
    return (cw_ref[0:1, cols] * _delayed(cur, hist_ref, cols, 2, step=step, prompt=prompt)
            + cw_ref[1:2, cols] * _delayed(cur, hist_ref, cols, 1, step=step, prompt=prompt)
            + cw_ref[2:3, cols] * cur)


def _save_tail(cur, hist_ref, st_ref, cols, *, step, prompt):
    rows = cur.shape[0]
    st_ref[:, cols] = cur[rows - (CONV_W - 1) * step:rows, :]
    if prompt:
        hist_ref[:, cols] = cur[rows - SUBLANES:rows, :]


def _sconv_part(x_ref, o_ref, hist_ref, st_ref, gn_ref, win_ref, cw_ref, wout_ref,
                h_s, z_s, *, step, prompt):
    d = x_ref.shape[1]
    h_s[...] = _rms(x_ref[...], gn_ref[...]).astype(_BF16)
    for c in range(d // LANE_TILE):
        cols = slice(c * LANE_TILE, (c + 1) * LANE_TILE)
        bg = _dot(h_s[...], win_ref[:, cols])
        cg = _dot(h_s[...], win_ref[:, d + c * LANE_TILE:d + (c + 1) * LANE_TILE])
        v = _dot(h_s[...], win_ref[:, 2 * d + c * LANE_TILE:2 * d + (c + 1) * LANE_TILE])
        u = cg * v
        y = _conv3(u, hist_ref, cw_ref, cols, step=step, prompt=prompt)
        z_s[:, cols] = (bg * y).astype(_BF16)
        _save_tail(u, hist_ref, st_ref, cols, step=step, prompt=prompt)
    o_ref[...] = x_ref[...] + _dot(z_s[...], wout_ref[...])


def _cmlp_prompt_part(x_ref, o_ref, gn_ref, win_ref, gv_ref, ws_ref, bias_ref, wout_ref,
                      h_s, z_s, vb_s, *, rows):
    d = x_ref.shape[1]
    n_heads = ws_ref.shape[0]
    head_dim = d // n_heads
    h_s[...] = _rms(x_ref[...], gn_ref[...]).astype(_BF16)
    vb_s[...] = _rms(_dot(h_s[...], win_ref[:, d:2 * d]), gv_ref[...]).astype(_BF16)

    t_idx = lax.broadcasted_iota(jnp.int32, (CHUNK, CHUNK), 0)
    s_idx = lax.broadcasted_iota(jnp.int32, (CHUNK, CHUNK), 1)
    causal = t_idx >= s_idx
    for hd in range(n_heads):
        cols = slice(hd * head_dim, (hd + 1) * head_dim)
        u = _dot(h_s[...], win_ref[:, cols])
        w_m = jnp.where(causal, ws_ref[hd], 0.0).astype(_BF16)
        bias = bias_ref[:, hd:hd + 1]
        for n in range(rows // CHUNK):
            rws = slice(n * CHUNK, (n + 1) * CHUNK)
            s = _dot(w_m, vb_s[rws, cols]) + bias
            z_s[rws, cols] = (u[rws, :] * s).astype(_BF16)
    o_ref[...] = x_ref[...] + _dot(z_s[...], wout_ref[...])


def _cmlp_sample_part(x_ref, o_ref, v_ref, wsm_ref, bsm_ref, gn_ref, win_ref, gv_ref, wout_ref,
                      h_s, z_s, *, step, rows):
    d = x_ref.shape[1]
    n_steps = rows // step
    n_heads = bsm_ref.shape[0] // n_steps
    head_dim = d // n_heads
    h_s[...] = _rms(x_ref[...], gn_ref[...]).astype(_BF16)
    v_ref[...] = _rms(_dot(h_s[...], win_ref[:, d:2 * d]), gv_ref[...])
    for hd in range(n_heads):
        cols = slice(hd * head_dim, (hd + 1) * head_dim)
        u = _dot(h_s[...], win_ref[:, cols])
        for t in range(n_steps):
            rws = slice(t * step, (t + 1) * step)
            s = jnp.full((step, head_dim), bsm_ref[hd * n_steps + t], _F32)
            for k in range(t + 1):
                w = wsm_ref[(hd * n_steps + t) * n_steps + k]
                s = s + w * v_ref[k * step:(k + 1) * step, cols]
            z_s[rws, cols] = (u[rws, :] * s).astype(_BF16)
    o_ref[...] = x_ref[...] + _dot(z_s[...], wout_ref[...])


def _pool_prompt_part(x_ref, o_ref, st_ref, gn_ref, wg_ref, sc_ref, hbuf, *, rows):
    hist = 2 * SUBLANES
    gdim = wg_ref.shape[1]
    j = pl.program_id(1)

    @pl.when(j == 0)
    def _():
        hbuf[0:hist, :] = jnp.zeros((hist, hbuf.shape[1]), _F32)

    @pl.when(j > 0)
    def _():
        hbuf[0:hist, :] = hbuf[rows:rows + hist, :]

    hbuf[hist:hist + rows, :] = _rms(x_ref[...], gn_ref[...])
    pos = j * rows + lax.broadcasted_iota(jnp.int32, (hist, 1), 0)
    for gi, w in enumerate(POOL_WINDOWS):
        cols = slice(gi * gdim, (gi + 1) * gdim)
        tot = hbuf[:, cols]
        span = 1
        while span < w:
            tot = tot + pltpu.roll(tot, span, 0)
            span *= 2
        tot = tot[hist:, :]
        h = hbuf[hist:hist + rows, cols]
        cnt = jnp.minimum(pos + 1, w).astype(_F32)
        mean = jnp.concatenate([tot[0:hist, :] / cnt, tot[hist:, :] * (1.0 / w)], axis=0)
        dlt = (mean - h).astype(_BF16)
        o_ref[:, cols] = x_ref[:, cols] + _dot(dlt, wg_ref[cols, :]) * sc_ref[:, cols]
    st_ref[...] = hbuf[hist + rows - POOL_STATE:hist + rows, :]


def _pool_sample_part(x_ref, o_ref, state_ref, st_ref, gn_ref, wg_ref, sc_ref, hbuf, *, step, rows):
    n_state = len(state_ref)
    n_new = rows // step
    gdim = wg_ref.shape[1]
    hbuf[...] = _rms(x_ref[...], gn_ref[...])
    for gi, w in enumerate(POOL_WINDOWS):
        cols = slice(gi * gdim, (gi + 1) * gdim)
        h = hbuf[:, cols]
        tot = h
        for k in range(1, w):
            past = [s[:, cols] for s in state_ref[n_state - k:n_state - k + n_new]]
            if k < n_new:
                past.append(hbuf[0:rows - k * step, cols])
            tot = tot + jnp.concatenate(past, axis=0)
        dlt = (tot * (1.0 / w) - h).astype(_BF16)
        o_ref[:, cols] = x_ref[:, cols] + _dot(dlt, wg_ref[cols, :]) * sc_ref[:, cols]
    for t in range(n_state - n_new):
        st_ref[t * step:(t + 1) * step, :] = state_ref[n_new + t][...]
    st_ref[(n_state - n_new) * step:n_state * step, :] = hbuf[...]


def _ffn_part(o_ref, hist_ref, st_ref, gn_ref, wup_ref, cw_ref, cb_ref, wdn_ref, gfin_ref,
              h_s, act_s, *, step, prompt):
    d_ff = wdn_ref.shape[0]
    h_s[...] = _rms(o_ref[...], gn_ref[...]).astype(_BF16)
    for c in range(d_ff // LANE_TILE):
        cols = slice(c * LANE_TILE, (c + 1) * LANE_TILE)
        g = _dot(h_s[...], wup_ref[:, cols])
        a = _dot(h_s[...], wup_ref[:, d_ff + c * LANE_TILE:d_ff + (c + 1) * LANE_TILE])
        y = _conv3(g, hist_ref, cw_ref, cols, step=step, prompt=prompt) + cb_ref[:, cols]
        act_s[:, cols] = (y * jax.nn.sigmoid(y) * a).astype(_BF16)
        _save_tail(g, hist_ref, st_ref, cols, step=step, prompt=prompt)
    out = o_ref[...] + _dot(act_s[...], wdn_ref[...])
    if gfin_ref is not None:
        out = _rms(out, gfin_ref[...])
    o_ref[...] = out


def _layer_kernel(*refs, kind, prompt, final_norm, step, rows, n_cast, n_grid):
    it = iter(refs)
    nxt = lambda n=1: next(it) if n == 1 else [next(it) for _ in range(n)]
    conv_kw = dict(step=step, prompt=prompt)
    if kind == 1 and not prompt:
        wsm_ref, bsm_ref = nxt(2)
    x_ref = nxt()
    if not prompt:
        n_mix_state = {0: CONV_W - 1, 1: 0, 2: POOL_STATE}[kind]
        mix_state = [nxt() for _ in range(n_mix_state)]
        ffn_state = [nxt() for _ in range(CONV_W - 1)]
    n_mix_w = {0: 4, 1: 6 if prompt else 4, 2: 3}[kind]
    mix_w = nxt(n_mix_w)
    gffn_ref, wup_hbm, fcw_ref, fcb_ref, wdn_hbm = nxt(5)
    gfin_ref = nxt() if final_norm else None
    cast_in = [nxt() for _ in range(n_cast)]
    o_ref = nxt()
    mix_out = nxt() if not (prompt and kind == 1) else None
    ffn_st = nxt()
    cast_out = [nxt() for _ in range(n_cast)]
    h_s, act_s, wup_v, wdn_v, w_sem = nxt(5)

    ffn_w = (gffn_ref, wup_v, fcw_ref, fcb_ref, wdn_v)
    ffn_copies = (pltpu.make_async_copy(wup_hbm, wup_v, w_sem.at[0]),
                  pltpu.make_async_copy(wdn_hbm, wdn_v, w_sem.at[1]))
    first_step = functools.reduce(jnp.logical_and, [pl.program_id(a) == 0 for a in range(n_grid)])

    @pl.when(first_step)
    def _():
        for cp in ffn_copies:
            cp.start()

    for src, dst in zip(cast_in, cast_out):
        dst[...] = src[...].astype(_BF16)
    if prompt:
        ffn_carry = nxt()

        @pl.when(pl.program_id(1) == 0)
        def _():
            ffn_carry[...] = jnp.zeros(ffn_carry.shape, _F32)
    else:
        ffn_carry = ffn_state

    if kind == 0:
        z_s = nxt()
        if prompt:
            mix_carry = nxt()

            @pl.when(pl.program_id(1) == 0)
            def _():
                mix_carry[...] = jnp.zeros(mix_carry.shape, _F32)
        else:
            mix_carry = mix_state
        _sconv_part(x_ref, o_ref, mix_carry, mix_out, *mix_w, h_s, z_s, **conv_kw)
    elif kind == 1:
        if prompt:
            z_s, vb_s = nxt(2)
            _cmlp_prompt_part(x_ref, o_ref, *mix_w, h_s, z_s, vb_s, rows=rows)
        else:
            z_s = nxt()
            _cmlp_sample_part(x_ref, o_ref, mix_out, wsm_ref, bsm_ref, *mix_w, h_s, z_s,
                              step=step, rows=rows)
    else:
        hbuf = nxt()
        if prompt:
            _pool_prompt_part(x_ref, o_ref, mix_out, *mix_w, hbuf, rows=rows)
        else:
            _pool_sample_part(x_ref, o_ref, mix_state, mix_out, *mix_w, hbuf, step=step, rows=rows)

    @pl.when(first_step)
    def _():
        for cp in ffn_copies:
            cp.wait()

    _ffn_part(o_ref, ffn_carry, ffn_st, *ffn_w, gfin_ref, h_s, act_s, **conv_kw)


def _layer(x, i, mix_state, ffn_state, w, wb, cast_next, *, prompt, group):
    n, d = x.shape
    d_ff = wb["f_w_down"].shape[0]
    depth = w["g_mix"].shape[0]
    kind, j = i % 3, i // 3
    final_norm = i == depth - 1
    rows = PROMPT_ROWS if prompt else TILE_ROWS
    if prompt:
        n_t = n // group // rows
        grid = (group, n_t)
        step = 1
        x_spec = pl.BlockSpec((rows, d), lambda b, t: (b * n_t + t, 0))

        def state_spec(n_rows, width):
            return pl.BlockSpec((None, n_rows, width), lambda b, t: (b, 0, 0))

        def state_shape(n_rows, width):
            return jax.ShapeDtypeStruct((group, n_rows, width), _F32)
    else:
        grid = (n // rows,)
        step = group
        x_spec = pl.BlockSpec((rows, d), lambda g: (g, 0))

        def state_spec(n_rows, width):
            return pl.BlockSpec((n_rows * step, width), lambda g: (g, 0))

        def state_shape(n_rows, width):
            return jax.ShapeDtypeStruct((grid[0] * n_rows * step, width), _F32)

        def add_steps(arr, layer):
            n_k = arr.shape[1] // (grid[0] * group)
            for k in range(n_k):
                add_in(arr, pl.BlockSpec((None, group, arr.shape[2]),
                                         lambda g, k=k: (layer, g * n_k + k, 0)))

    ins, in_specs, outs, out_specs, scratch = [], [], [], [], []

    def add_in(arr, spec):
        ins.append(arr)
        in_specs.append(spec)

    def add_w(name, layer):
        if name in wb:
            add_in(wb[name], _layer_spec(wb[name], None))
        else:
            add_in(w[name], _layer_spec(w[name], layer))

    if kind == 1 and not prompt:
        n_steps = rows // step
        smem = pl.BlockSpec(memory_space=pltpu.SMEM)
        add_in(w["b_w_s"][j, :, :n_steps, :n_steps].reshape(-1), smem)
        add_in(w["b_bias"][j, :, :n_steps].reshape(-1), smem)
    add_in(x, x_spec)
    if not prompt:
        if kind != 1:
            add_steps(mix_state, j)
        add_steps(ffn_state, i)
    add_w("g_mix", i)
    if kind == 0:
        for name in ("a_w_in", "a_conv", "a_w_out"):
            add_w(name, j)
    elif kind == 1:
        names = ("b_w_in", "b_g_v", "b_w_s", "b_bias_t", "b_w_out") if prompt else (
            "b_w_in", "b_g_v", "b_w_out")
        for name in names:
            add_w(name, j)
    else:
        for name in ("c_w_group", "c_scale"):
            add_w(name, j)
    for name in ("g_ffn", "f_w_up", "f_conv", "f_conv_b", "f_w_down"):
        if name in wb:
            add_in(wb[name], pl.BlockSpec(memory_space=pl.ANY))
        else:
            add_w(name, i)
    if final_norm:
        add_in(w["g_final"], pl.BlockSpec(w["g_final"].shape, lambda *_: (0, 0)))
    cast_specs = []
    for name, stack, idx in cast_next:
        n_rows, width = stack.shape[1:]
        blk_rows = _cast_rows(n_rows, n // rows)
        last = n_rows // blk_rows - 1
        add_in(stack, pl.BlockSpec(
            (None, blk_rows, width),
            lambda b, t, idx=idx, last=last: (idx, jnp.minimum(b * n_t + t, last), 0)))
        cast_specs.append(pl.BlockSpec(
            (blk_rows, width), lambda b, t, last=last: (jnp.minimum(b * n_t + t, last), 0)))

    outs.append(jax.ShapeDtypeStruct((n, d), _F32))
    out_specs.append(x_spec)
    if kind == 0:
        outs.append(state_shape(CONV_W - 1, d))
        out_specs.append(state_spec(CONV_W - 1, d))
    elif kind == 1 and not prompt:
        outs.append(jax.ShapeDtypeStruct((n, d), _F32))
        out_specs.append(x_spec)
    elif kind == 2:
        outs.append(state_shape(POOL_STATE, d))
        out_specs.append(state_spec(POOL_STATE, d))
    outs.append(state_shape(CONV_W - 1, d_ff))
    out_specs.append(state_spec(CONV_W - 1, d_ff))
    n_fixed_outs = len(outs)
    for (name, stack, idx), spec in zip(cast_next, cast_specs):
        outs.append(jax.ShapeDtypeStruct(stack.shape[1:], _BF16))
        out_specs.append(spec)

    scratch += [pltpu.VMEM((rows, d), _BF16), pltpu.VMEM((rows, d_ff), _BF16),
                pltpu.VMEM(wb["f_w_up"].shape, _BF16), pltpu.VMEM(wb["f_w_down"].shape, _BF16),
                pltpu.SemaphoreType.DMA((2,))]
    if prompt:
        scratch.append(pltpu.VMEM((SUBLANES, d_ff), _F32))
    if kind == 0:
        scratch.append(pltpu.VMEM((rows, d), _BF16))
        if prompt:
            scratch.append(pltpu.VMEM((SUBLANES, d), _F32))
    elif kind == 1:
        scratch.append(pltpu.VMEM((rows, d), _BF16))
        if prompt:
            scratch.append(pltpu.VMEM((rows, d), _BF16))
    else:
        scratch.append(pltpu.VMEM((2 * SUBLANES + rows if prompt else rows, d), _F32))

    res = pl.pallas_call(
        functools.partial(_layer_kernel, kind=kind, prompt=prompt, final_norm=final_norm,
                          step=step, rows=rows, n_cast=len(cast_next), n_grid=len(grid)),
        grid=grid,
        in_specs=in_specs,
        out_specs=out_specs,
        out_shape=outs,
        scratch_shapes=scratch,
        compiler_params=pltpu.CompilerParams(
            dimension_semantics=("arbitrary",) * len(grid),
            vmem_limit_bytes=VMEM_LIMIT_BYTES),
        name=f"layer{i}_{'prompt' if prompt else 'sample'}",
    )(*ins)
    x_new, ffn_new = res[0], res[n_fixed_outs - 1]
    mix_new = res[1] if n_fixed_outs == 3 else None
    wb_next = {name: arr for (name, _, _), arr in zip(cast_next, res[n_fixed_outs:])}
    return x_new, mix_new, ffn_new, wb_next


def _trunk(x, conv_prev, pool_prev, ffn_prev, w, wbs, big, *, prompt, group):
    depth = w["g_mix"].shape[0]
    new = {0: [], 1: [], 2: []}
    new_ffn = []
    for i in range(depth):
        kind = i % 3
        mix_state = conv_prev if kind == 0 else pool_prev
        cast_next = []
        if prompt and i + 1 < depth:
            cast_next = [(name, big[name], idx) for name, idx in _big_weights(i + 1)]
        x, mix_new, ffn_new, wb_next = _layer(x, i, mix_state, ffn_prev, w, wbs[i], cast_next,
                                              prompt=prompt, group=group)
        if cast_next:
            wbs.append(wb_next)
        if mix_new is not None:
            new[kind].append(mix_new)
        new_ffn.append(ffn_new)
    return x, jnp.stack(new[0]), jnp.stack(new[2]), jnp.stack(new_ffn), new[1]


def _cast_kernel(*refs):
    n = len(refs) // 2
    for src, dst in zip(refs[:n], refs[n:]):
        dst[...] = src[...].astype(_BF16)


def _cast_layer0(big):
    n_steps = 8
    ins, in_specs, outs, out_specs = [], [], [], []
    for name, idx in _big_weights(0):
        stack = big[name]
        n_rows, width = stack.shape[1:]
        blk_rows = _cast_rows(n_rows, n_steps)
        last = n_rows // blk_rows - 1
        ins.append(stack)
        in_specs.append(pl.BlockSpec((None, blk_rows, width),
                                     lambda s, idx=idx, last=last: (idx, jnp.minimum(s, last), 0)))
        outs.append(jax.ShapeDtypeStruct((n_rows, width), _BF16))
        out_specs.append(pl.BlockSpec((blk_rows, width),
                                      lambda s, last=last: (jnp.minimum(s, last), 0)))
    res = pl.pallas_call(
        _cast_kernel, grid=(n_steps,), in_specs=in_specs, out_specs=out_specs, out_shape=outs,
        compiler_params=pltpu.CompilerParams(dimension_semantics=("arbitrary",),
                                             vmem_limit_bytes=VMEM_LIMIT_BYTES),
        name="cast_layer0",
    )(*ins)
    return {name: arr for (name, _), arr in zip(_big_weights(0), res)}


def _to_groups(a, group):
    *lead, b, k, c = a.shape
    a = a.reshape(*lead, b // group, group, k, c)
    a = jnp.swapaxes(a, -3, -2)
    return a.reshape(*lead, b * k, c)


def _from_groups(a, group, k):
    *lead, n, c = a.shape
    a = a.reshape(*lead, n // (k * group), k, group, c)
    a = jnp.swapaxes(a, -3, -2)
    return a.reshape(*lead, n // k, k, c)


def kernel(x_prompt, x_sample, state_shortconv, state_pool, state_ffnconv, g_mix, g_ffn, g_final, a_w_in, a_conv, a_w_out, b_w_in, b_g_v, b_w_s, b_bias, b_w_out, c_w_group, c_scale, f_w_up, f_conv, f_conv_b, f_w_down):
    batch, seq, d = x_prompt.shape
    dec_batch, dec_seq, _ = x_sample.shape
    row = lambda a: a[:, None, :]
    w = dict(g_mix=row(g_mix), g_ffn=row(g_ffn), g_final=g_final[None, :],
             a_conv=a_conv, b_g_v=row(b_g_v), b_w_s=b_w_s, b_bias=b_bias,
             b_bias_t=jnp.swapaxes(b_bias, 1, 2), c_scale=row(c_scale),
             f_conv=f_conv, f_conv_b=row(f_conv_b))
    big = dict(a_w_in=a_w_in, a_w_out=a_w_out, b_w_in=b_w_in, b_w_out=b_w_out,
               c_w_group=c_w_group.reshape(c_w_group.shape[0], -1, c_w_group.shape[-1]),
               f_w_up=f_w_up, f_w_down=f_w_down)
    wbs = [_cast_layer0(big)]

    yp, conv_p, pool_p, ffn_p, _ = _trunk(
        x_prompt.reshape(batch * seq, d), None, None, None, w, wbs, big, prompt=True, group=batch)

    group = TILE_ROWS // dec_seq
    tg = functools.partial(_to_groups, group=group)
    ys, conv_s, pool_s, ffn_s, v_s = _trunk(
        tg(x_sample), tg(state_shortconv), tg(state_pool), tg(state_ffnconv), w, wbs, big,
        prompt=False, group=group)
    fg = functools.partial(_from_groups, group=group)

    return (yp.reshape(batch, seq, d),
            fg(ys, k=dec_seq),
            conv_p, fg(conv_s, k=CONV_W - 1),
            pool_p, fg(pool_s, k=POOL_STATE),
            ffn_p, fg(ffn_s, k=CONV_W - 1),
            fg(jnp.stack(v_s), k=dec_seq))
```

```python
import functools

import jax
import jax.numpy as jnp
from jax import lax
from jax.experimental import pallas as pl
from jax.experimental.pallas import tpu as pltpu

EPS = 1e-6
CONV_W = 3
CHUNK = 128
POOL_WINDOWS = (2, 4, 8, 16)
POOL_STATE = max(POOL_WINDOWS) - 1
assert all(w & (w - 1) == 0 for w in POOL_WINDOWS)
POOL_HIST = 16
POOL_PARTS = 2
POOL_STAGES = 1 + len(POOL_WINDOWS)

LANE_TILE = 256
SUBLANES = 8
PROMPT_ROWS = 1024
TILE_ROWS = 512
VMEM_LIMIT_BYTES = 60 * 1024 * 1024

_BF16 = jnp.bfloat16
_F32 = jnp.float32


def _rms(x, g):
    return x * lax.rsqrt(jnp.mean(x * x, axis=-1, keepdims=True) + EPS) * g


def _dot(a, b):
    return jnp.dot(a, b, preferred_element_type=_F32)


def _layer_spec(arr, layer):
    if layer is None:
        blk, idx = arr.shape, (0,) * arr.ndim
    else:
        blk, idx = (None,) + arr.shape[1:], (layer,) + (0,) * (arr.ndim - 1)
    return pl.BlockSpec(blk, lambda *_: idx, pipeline_mode=pl.Buffered(1))


def _big_weights(i):
    kind, j = i % 3, i // 3
    mixer = {0: ("a_w_in", "a_w_out"), 1: ("b_w_in", "b_w_out"), 2: ("c_w_group",)}[kind]
    return [(name, j) for name in mixer] + [("f_w_up", i), ("f_w_down", i)]


def _cast_rows(n_rows, n_steps):
    bf16_rows = 2 * SUBLANES
    if n_rows % (n_steps * bf16_rows) == 0:
        return n_rows // n_steps
    return CHUNK


def _delayed(cur, hist_ref, cols, k, *, step, prompt):
    rows = cur.shape[0]
    if prompt:
        rolled = pltpu.roll(cur, k, 0)
        row = lax.broadcasted_iota(jnp.int32, (SUBLANES, cur.shape[1]), 0)
        top = jnp.where(row < k, pltpu.roll(hist_ref[:, cols], k, 0), rolled[0:SUBLANES, :])
        return jnp.concatenate([top, rolled[SUBLANES:, :]], axis=0)
    past = [h[:, cols] for h in hist_ref[len(hist_ref) - k:]]
    return jnp.concatenate(past + [cur[0:rows - k * step, :]], axis=0)


def _conv3(cur, hist_ref, cw_ref, cols, *, step, prompt):
    return (cw_ref[0:1, cols] * _delayed(cur, hist_ref, cols, 2, step=step, prompt=prompt)
            + cw_ref[1:2, cols] * _delayed(cur, hist_ref, cols, 1, step=step, prompt=prompt)
            + cw_ref[2:3, cols] * cur)


def _save_tail(cur, hist_ref, st_ref, cols, *, step, prompt):
    rows = cur.shape[0]
    st_ref[:, cols] = cur[rows - (CONV_W - 1) * step:rows, :]
    if prompt:
        hist_ref[:, cols] = cur[rows - SUBLANES:rows, :]


def _sconv_part(x_ref, o_ref, hist_ref, st_ref, gn_ref, win_ref, cw_ref, wout_ref,
                h_s, z_s, *, step, prompt):
    d = x_ref.shape[1]
    h_s[...] = _rms(x_ref[...], gn_ref[...]).astype(_BF16)
    for c in range(d // LANE_TILE):
        cols = slice(c * LANE_TILE, (c + 1) * LANE_TILE)
        bg = _dot(h_s[...], win_ref[:, cols])
        cg = _dot(h_s[...], win_ref[:, d + c * LANE_TILE:d + (c + 1) * LANE_TILE])
        v = _dot(h_s[...], win_ref[:, 2 * d + c * LANE_TILE:2 * d + (c + 1) * LANE_TILE])
        u = cg * v
        y = _conv3(u, hist_ref, cw_ref, cols, step=step, prompt=prompt)
        z_s[:, cols] = (bg * y).astype(_BF16)
        _save_tail(u, hist_ref, st_ref, cols, step=step, prompt=prompt)
    o_ref[...] = x_ref[...] + _dot(z_s[...], wout_ref[...])


def _cmlp_prompt_part(x_ref, o_ref, gn_ref, win_ref, gv_ref, ws_ref, bias_ref, wout_ref,
                      h_s, z_s, vb_s, *, rows):
    d = x_ref.shape[1]
    n_heads = ws_ref.shape[0]
    head_dim = d // n_heads
    h_s[...] = _rms(x_ref[...], gn_ref[...]).astype(_BF16)
    vb_s[...] = _rms(_dot(h_s[...], win_ref[:, d:2 * d]), gv_ref[...]).astype(_BF16)

    t_idx = lax.broadcasted_iota(jnp.int32, (CHUNK, CHUNK), 0)
    s_idx = lax.broadcasted_iota(jnp.int32, (CHUNK, CHUNK), 1)
    causal = t_idx >= s_idx
    for hd in range(n_heads):
        cols = slice(hd * head_dim, (hd + 1) * head_dim)
        u = _dot(h_s[...], win_ref[:, cols])
        w_m = jnp.where(causal, ws_ref[hd], 0.0).astype(_BF16)
        bias = bias_ref[:, hd:hd + 1]
        for n in range(rows // CHUNK):
            rws = slice(n * CHUNK, (n + 1) * CHUNK)
            s = _dot(w_m, vb_s[rws, cols]) + bias
            z_s[rws, cols] = (u[rws, :] * s).astype(_BF16)
    o_ref[...] = x_ref[...] + _dot(z_s[...], wout_ref[...])


def _cmlp_sample_part(x_ref, o_ref, v_ref, wsm_ref, bsm_ref, gn_ref, win_ref, gv_ref, wout_ref,
                      h_s, z_s, *, step, rows):
    d = x_ref.shape[1]
    n_steps = rows // step
    n_heads = bsm_ref.shape[0] // n_steps
    head_dim = d // n_heads
    h_s[...] = _rms(x_ref[...], gn_ref[...]).astype(_BF16)
    v_ref[...] = _rms(_dot(h_s[...], win_ref[:, d:2 * d]), gv_ref[...])
    for hd in range(n_heads):
        cols = slice(hd * head_dim, (hd + 1) * head_dim)
        u = _dot(h_s[...], win_ref[:, cols])
        for t in range(n_steps):
            rws = slice(t * step, (t + 1) * step)
            s = jnp.full((step, head_dim), bsm_ref[hd * n_steps + t], _F32)
            for k in range(t + 1):
                w = wsm_ref[(hd * n_steps + t) * n_steps + k]
                s = s + w * v_ref[k * step:(k + 1) * step, cols]
            z_s[rws, cols] = (u[rws, :] * s).astype(_BF16)
    o_ref[...] = x_ref[...] + _dot(z_s[...], wout_ref[...])


def _pool_prompt_stages(x_ref, o_ref, st_ref, gn_ref, wg_ref, sc_ref, hbuf, *, pos0):
    hist = POOL_HIST
    rows = x_ref.shape[0]
    gdim = wg_ref.shape[1]
    hbuf[hist:hist + rows, :] = _rms(x_ref[...], gn_ref[...])
    st_ref[...] = hbuf[hist + rows - POOL_STATE:hist + rows, :]
    yield
    pos = pos0 + lax.broadcasted_iota(jnp.int32, (hist, 1), 0)
    for gi, w in enumerate(POOL_WINDOWS):
        cols = slice(gi * gdim, (gi + 1) * gdim)
        tot = hbuf[:, cols]
        span = 1
        while span < w:
            tot = tot + pltpu.roll(tot, span, 0)
            span *= 2
        tot = tot[hist:, :]
        h = hbuf[hist:hist + rows, cols]
        cnt = jnp.minimum(pos + 1, w).astype(_F32)
        mean = jnp.concatenate([tot[0:hist, :] / cnt, tot[hist:, :] * (1.0 / w)], axis=0)
        dlt = (mean - h).astype(_BF16)
        o_ref[:, cols] = x_ref[:, cols] + _dot(dlt, wg_ref[cols, :]) * sc_ref[:, cols]
        yield


def _pool_sample_part(x_ref, o_ref, state_ref, st_ref, gn_ref, wg_ref, sc_ref, hbuf, *, step, rows):
    n_state = len(state_ref)
    n_new = rows // step
    gdim = wg_ref.shape[1]
    hbuf[...] = _rms(x_ref[...], gn_ref[...])
    for gi, w in enumerate(POOL_WINDOWS):
        cols = slice(gi * gdim, (gi + 1) * gdim)
        h = hbuf[:, cols]
        tot = h
        for k in range(1, w):
            past = [s[:, cols] for s in state_ref[n_state - k:n_state - k + n_new]]
            if k < n_new:
                past.append(hbuf[0:rows - k * step, cols])
            tot = tot + jnp.concatenate(past, axis=0)
        dlt = (tot * (1.0 / w) - h).astype(_BF16)
        o_ref[:, cols] = x_ref[:, cols] + _dot(dlt, wg_ref[cols, :]) * sc_ref[:, cols]
    for t in range(n_state - n_new):
        st_ref[t * step:(t + 1) * step, :] = state_ref[n_new + t][...]
    st_ref[(n_state - n_new) * step:n_state * step, :] = hbuf[...]


def _ffn_stages(o_ref, hist_ref, st_ref, gn_ref, wup_ref, cw_ref, cb_ref, wdn_ref, gfin_ref,
                h_s, act_s, *, step, prompt):
    d_ff = wdn_ref.shape[0]
    h_s[...] = _rms(o_ref[...], gn_ref[...]).astype(_BF16)
    for c in range(d_ff // LANE_TILE):
        cols = slice(c * LANE_TILE, (c + 1) * LANE_TILE)
        g = _dot(h_s[...], wup_ref[:, cols])
        a = _dot(h_s[...], wup_ref[:, d_ff + c * LANE_TILE:d_ff + (c + 1) * LANE_TILE])
        y = _conv3(g, hist_ref, cw_ref, cols, step=step, prompt=prompt) + cb_ref[:, cols]
        act_s[:, cols] = (y * jax.nn.sigmoid(y) * a).astype(_BF16)
        _save_tail(g, hist_ref, st_ref, cols, step=step, prompt=prompt)
        yield
    out = o_ref[...] + _dot(act_s[...], wdn_ref[...])
    if gfin_ref is not None:
        out = _rms(out, gfin_ref[...])
    o_ref[...] = out
    yield


def _ffn_part(*args, **kwargs):
    for _ in _ffn_stages(*args, **kwargs):
        pass


def _interleave(main, n_main, fill, n_fill):
    before = [(k * n_main) // n_fill for k in range(n_fill)]
    for i in range(n_main):
        for _ in range(before.count(i)):
            next(fill)
        next(main)


def _layer_kernel(*refs, kind, prompt, final_norm, step, rows, n_cast, n_grid):
    it = iter(refs)
    nxt = lambda n=1: next(it) if n == 1 else [next(it) for _ in range(n)]
    conv_kw = dict(step=step, prompt=prompt)
    if kind == 1 and not prompt:
        wsm_ref, bsm_ref = nxt(2)
    x_ref = nxt()
    if not prompt:
        n_mix_state = {0: CONV_W - 1, 1: 0, 2: POOL_STATE}[kind]
        mix_state = [nxt() for _ in range(n_mix_state)]
        ffn_state = [nxt() for _ in range(CONV_W - 1)]
    n_mix_w = {0: 4, 1: 6 if prompt else 4, 2: 3}[kind]
    mix_w = nxt(n_mix_w)
    gffn_ref, wup_hbm, fcw_ref, fcb_ref, wdn_hbm = nxt(5)
    gfin_ref = nxt() if final_norm else None
    cast_in = [nxt() for _ in range(n_cast)]
    o_ref = nxt()
    mix_out = nxt() if not (prompt and kind == 1) else None
    ffn_st = nxt()
    cast_out = [nxt() for _ in range(n_cast)]
    h_s, act_s, wup_v, wdn_v, w_sem = nxt(5)

    ffn_w = (gffn_ref, wup_v, fcw_ref, fcb_ref, wdn_v)
    ffn_copies = (pltpu.make_async_copy(wup_hbm, wup_v, w_sem.at[0]),
                  pltpu.make_async_copy(wdn_hbm, wdn_v, w_sem.at[1]))
    first_step = functools.reduce(jnp.logical_and, [pl.program_id(a) == 0 for a in range(n_grid)])

    @pl.when(first_step)
    def _():
        for cp in ffn_copies:
            cp.start()

    for src, dst in zip(cast_in, cast_out):
        dst[...] = src[...].astype(_BF16)
    if prompt:
        ffn_carry = nxt()

        @pl.when(pl.program_id(1) == 0)
        def _():
            ffn_carry[...] = jnp.zeros(ffn_carry.shape, _F32)
    else:
        ffn_carry = ffn_state

    if kind == 0:
        z_s = nxt()
        if prompt:
            mix_carry = nxt()

            @pl.when(pl.program_id(1) == 0)
            def _():
                mix_carry[...] = jnp.zeros(mix_carry.shape, _F32)
        else:
            mix_carry = mix_state
        _sconv_part(x_ref, o_ref, mix_carry, mix_out, *mix_w, h_s, z_s, **conv_kw)
    elif kind == 1:
        if prompt:
            z_s, vb_s = nxt(2)
            _cmlp_prompt_part(x_ref, o_ref, *mix_w, h_s, z_s, vb_s, rows=rows)
        else:
            z_s = nxt()
            _cmlp_sample_part(x_ref, o_ref, mix_out, wsm_ref, bsm_ref, *mix_w, h_s, z_s,
                              step=step, rows=rows)
    elif prompt:
        hbuf = nxt()
        t_seq = pl.program_id(1)

        @pl.when(t_seq == 0)
        def _():
            hbuf[0:POOL_HIST, :] = jnp.zeros((POOL_HIST, hbuf.shape[1]), _F32)

        @pl.when(t_seq > 0)
        def _():
            hbuf[0:POOL_HIST, :] = hbuf[rows:rows + POOL_HIST, :]

        part = rows // POOL_PARTS
        n_ffn_stages = act_s.shape[1] // LANE_TILE + 1

        def pool(p):
            rws = pl.ds(p * part, part)
            return _pool_prompt_stages(x_ref.at[rws], o_ref.at[rws], mix_out, *mix_w,
                                       hbuf.at[pl.ds(p * part, POOL_HIST + part)],
                                       pos0=t_seq * rows + p * part)

        def ffn(p):
            rws = pl.ds(p * part, part)
            return _ffn_stages(o_ref.at[rws], ffn_carry, ffn_st, *ffn_w, gfin_ref, h_s.at[rws],
                               act_s.at[rws], **conv_kw)

        for _ in pool(0):
            pass

        @pl.when(first_step)
        def _():
            for cp in ffn_copies:
                cp.wait()

        for p in range(POOL_PARTS - 1):
            _interleave(ffn(p), n_ffn_stages, pool(p + 1), POOL_STAGES)
        for _ in ffn(POOL_PARTS - 1):
            pass
        return
    else:
        hbuf = nxt()
        _pool_sample_part(x_ref, o_ref, mix_state, mix_out, *mix_w, hbuf, step=step, rows=rows)

    @pl.when(first_step)
    def _():
        for cp in ffn_copies:
            cp.wait()

    _ffn_part(o_ref, ffn_carry, ffn_st, *ffn_w, gfin_ref, h_s, act_s, **conv_kw)


def _layer(x, i, mix_state, ffn_state, w, wb, cast_next, *, prompt, group):
    n, d = x.shape
    d_ff = wb["f_w_down"].shape[0]
    depth = w["g_mix"].shape[0]
    kind, j = i % 3, i // 3
    final_norm = i == depth - 1
    rows = PROMPT_ROWS if prompt else TILE_ROWS
    if prompt:
        n_t = n // group // rows
        grid = (group, n_t)
        step = 1
        x_spec = pl.BlockSpec((rows, d), lambda b, t: (b * n_t + t, 0))

        def state_spec(n_rows, width):
            return pl.BlockSpec((None, n_rows, width), lambda b, t: (b, 0, 0))

        def state_shape(n_rows, width):
            return jax.ShapeDtypeStruct((group, n_rows, width), _F32)
    else:
        grid = (n // rows,)
        step = group
        x_spec = pl.BlockSpec((rows, d), lambda g: (g, 0))

        def state_spec(n_rows, width):
            return pl.BlockSpec((n_rows * step, width), lambda g: (g, 0))

        def state_shape(n_rows, width):
            return jax.ShapeDtypeStruct((grid[0] * n_rows * step, width), _F32)

        def add_steps(arr, layer):
            n_k = arr.shape[1] // (grid[0] * group)
            for k in range(n_k):
                add_in(arr, pl.BlockSpec((None, group, arr.shape[2]),
                                         lambda g, k=k: (layer, g * n_k + k, 0)))

    ins, in_specs, outs, out_specs, scratch = [], [], [], [], []

    def add_in(arr, spec):
        ins.append(arr)
        in_specs.append(spec)

    def add_w(name, layer):
        if name in wb:
            add_in(wb[name], _layer_spec(wb[name], None))
        else:
            add_in(w[name], _layer_spec(w[name], layer))

    if kind == 1 and not prompt:
        n_steps = rows // step
        smem = pl.BlockSpec(memory_space=pltpu.SMEM)
        add_in(w["b_w_s"][j, :, :n_steps, :n_steps].reshape(-1), smem)
        add_in(w["b_bias"][j, :, :n_steps].reshape(-1), smem)
    add_in(x, x_spec)
    if not prompt:
        if kind != 1:
            add_steps(mix_state, j)
        add_steps(ffn_state, i)
    add_w("g_mix", i)
    if kind == 0:
        for name in ("a_w_in", "a_conv", "a_w_out"):
            add_w(name, j)
    elif kind == 1:
        names = ("b_w_in", "b_g_v", "b_w_s", "b_bias_t", "b_w_out") if prompt else (
            "b_w_in", "b_g_v", "b_w_out")
        for name in names:
            add_w(name, j)
    else:
        for name in ("c_w_group", "c_scale"):
            add_w(name, j)
    for name in ("g_ffn", "f_w_up", "f_conv", "f_conv_b", "f_w_down"):
        if name in wb:
            add_in(wb[name], pl.BlockSpec(memory_space=pl.ANY))
        else:
            add_w(name, i)
    if final_norm:
        add_in(w["g_final"], pl.BlockSpec(w["g_final"].shape, lambda *_: (0, 0)))
    cast_specs = []
    for name, stack, idx in cast_next:
        n_rows, width = stack.shape[1:]
        blk_rows = _cast_rows(n_rows, n // rows)
        last = n_rows // blk_rows - 1
        add_in(stack, pl.BlockSpec(
            (None, blk_rows, width),
            lambda b, t, idx=idx, last=last: (idx, jnp.minimum(b * n_t + t, last), 0)))
        cast_specs.append(pl.BlockSpec(
            (blk_rows, width), lambda b, t, last=last: (jnp.minimum(b * n_t + t, last), 0)))

    outs.append(jax.ShapeDtypeStruct((n, d), _F32))
    out_specs.append(x_spec)
    if kind == 0:
        outs.append(state_shape(CONV_W - 1, d))
        out_specs.append(state_spec(CONV_W - 1, d))
    elif kind == 1 and not prompt:
        outs.append(jax.ShapeDtypeStruct((n, d), _F32))
        out_specs.append(x_spec)
    elif kind == 2:
        outs.append(state_shape(POOL_STATE, d))
        out_specs.append(state_spec(POOL_STATE, d))
    outs.append(state_shape(CONV_W - 1, d_ff))
    out_specs.append(state_spec(CONV_W - 1, d_ff))
    n_fixed_outs = len(outs)
    for (name, stack, idx), spec in zip(cast_next, cast_specs):
        outs.append(jax.ShapeDtypeStruct(stack.shape[1:], _BF16))
        out_specs.append(spec)

    scratch += [pltpu.VMEM((rows, d), _BF16), pltpu.VMEM((rows, d_ff), _BF16),
                pltpu.VMEM(wb["f_w_up"].shape, _BF16), pltpu.VMEM(wb["f_w_down"].shape, _BF16),
                pltpu.SemaphoreType.DMA((2,))]
    if prompt:
        scratch.append(pltpu.VMEM((SUBLANES, d_ff), _F32))
    if kind == 0:
        scratch.append(pltpu.VMEM((rows, d), _BF16))
        if prompt:
            scratch.append(pltpu.VMEM((SUBLANES, d), _F32))
    elif kind == 1:
        scratch.append(pltpu.VMEM((rows, d), _BF16))
        if prompt:
            scratch.append(pltpu.VMEM((rows, d), _BF16))
    else:
        scratch.append(pltpu.VMEM((POOL_HIST + rows if prompt else rows, d), _F32))

    res = pl.pallas_call(
        functools.partial(_layer_kernel, kind=kind, prompt=prompt, final_norm=final_norm,
                          step=step, rows=rows, n_cast=len(cast_next), n_grid=len(grid)),
        grid=grid,
        in_specs=in_specs,
        out_specs=out_specs,
        out_shape=outs,
        scratch_shapes=scratch,
        compiler_params=pltpu.CompilerParams(
            dimension_semantics=("arbitrary",) * len(grid),
            vmem_limit_bytes=VMEM_LIMIT_BYTES),
        name=f"layer{i}_{'prompt' if prompt else 'sample'}",
    )(*ins)
    x_new, ffn_new = res[0], res[n_fixed_outs - 1]
    mix_new = res[1] if n_fixed_outs == 3 else None
    wb_next = {name: arr for (name, _, _), arr in zip(cast_next, res[n_fixed_outs:])}
    return x_new, mix_new, ffn_new, wb_next


def _trunk(x, conv_prev, pool_prev, ffn_prev, w, wbs, big, *, prompt, group):
    depth = w["g_mix"].shape[0]
    new = {0: [], 1: [], 2: []}
    new_ffn = []
    for i in range(depth):
        kind = i % 3
        mix_state = conv_prev if kind == 0 else pool_prev
        cast_next = []
        if prompt and i + 1 < depth:
            cast_next = [(name, big[name], idx) for name, idx in _big_weights(i + 1)]
        x, mix_new, ffn_new, wb_next = _layer(x, i, mix_state, ffn_prev, w, wbs[i], cast_next,
                                              prompt=prompt, group=group)
        if cast_next:
            wbs.append(wb_next)
        if mix_new is not None:
            new[kind].append(mix_new)
        new_ffn.append(ffn_new)
    return x, jnp.stack(new[0]), jnp.stack(new[2]), jnp.stack(new_ffn), new[1]


def _cast_kernel(*refs):
    n = len(refs) // 2
    for src, dst in zip(refs[:n], refs[n:]):
        dst[...] = src[...].astype(_BF16)


def _cast_layer0(big):
    n_steps = 8
    ins, in_specs, outs, out_specs = [], [], [], []
    for name, idx in _big_weights(0):
        stack = big[name]
        n_rows, width = stack.shape[1:]
        blk_rows = _cast_rows(n_rows, n_steps)
        last = n_rows // blk_rows - 1
        ins.append(stack)
        in_specs.append(pl.BlockSpec((None, blk_rows, width),
                                     lambda s, idx=idx, last=last: (idx, jnp.minimum(s, last), 0)))
        outs.append(jax.ShapeDtypeStruct((n_rows, width), _BF16))
        out_specs.append(pl.BlockSpec((blk_rows, width),
                                      lambda s, last=last: (jnp.minimum(s, last), 0)))
    res = pl.pallas_call(
        _cast_kernel, grid=(n_steps,), in_specs=in_specs, out_specs=out_specs, out_shape=outs,
        compiler_params=pltpu.CompilerParams(dimension_semantics=("arbitrary",),
                                             vmem_limit_bytes=VMEM_LIMIT_BYTES),
        name="cast_layer0",
    )(*ins)
    return {name: arr for (name, _), arr in zip(_big_weights(0), res)}


def _to_groups(a, group):
    *lead, b, k, c = a.shape
    a = a.reshape(*lead, b // group, group, k, c)
    a = jnp.swapaxes(a, -3, -2)
    return a.reshape(*lead, b * k, c)


def _from_groups(a, group, k):
    *lead, n, c = a.shape
    a = a.reshape(*lead, n // (k * group), k, group, c)
    a = jnp.swapaxes(a, -3, -2)
    return a.reshape(*lead, n // k, k, c)


def kernel(x_prompt, x_sample, state_shortconv, state_pool, state_ffnconv, g_mix, g_ffn, g_final, a_w_in, a_conv, a_w_out, b_w_in, b_g_v, b_w_s, b_bias, b_w_out, c_w_group, c_scale, f_w_up, f_conv, f_conv_b, f_w_down):
    batch, seq, d = x_prompt.shape
    dec_batch, dec_seq, _ = x_sample.shape
    row = lambda a: a[:, None, :]
    w = dict(g_mix=row(g_mix), g_ffn=row(g_ffn), g_final=g_final[None, :],
             a_conv=a_conv, b_g_v=row(b_g_v), b_w_s=b_w_s, b_bias=b_bias,
             b_bias_t=jnp.swapaxes(b_bias, 1, 2), c_scale=row(c_scale),
             f_conv=f_conv, f_conv_b=row(f_conv_b))
    big = dict(a_w_in=a_w_in, a_w_out=a_w_out, b_w_in=b_w_in, b_w_out=b_w_out,
               c_w_group=c_w_group.reshape(c_w_group.shape[0], -1, c_w_group.shape[-1]),
               f_w_up=f_w_up, f_w_down=f_w_down)
    wbs = [_cast_layer0(big)]

    yp, conv_p, pool_p, ffn_p, _ = _trunk(
        x_prompt.reshape(batch * seq, d), None, None, None, w, wbs, big, prompt=True, group=batch)

    group = TILE_ROWS // dec_seq
    tg = functools.partial(_to_groups, group=group)
    ys, conv_s, pool_s, ffn_s, v_s = _trunk(
        tg(x_sample), tg(state_shortconv), tg(state_pool), tg(state_ffnconv), w, wbs, big,
        prompt=False, group=group)
    fg = functools.partial(_from_groups, group=group)

    return (yp.reshape(batch, seq, d),
            fg(ys, k=dec_seq),
            conv_p, fg(conv_s, k=CONV_W - 1),
            pool_p, fg(pool_s, k=POOL_STATE),
            ffn_p, fg(ffn_s, k=CONV_W - 1),
            fg(jnp.stack(v_s), k=dec_seq))
```

```python
import functools

import jax
import jax.numpy as jnp
from jax import lax
from jax.experimental import pallas as pl
from jax.experimental.pallas import tpu as pltpu

EPS = 1e-6
CONV_W = 3
CHUNK = 128
POOL_WINDOWS = (2, 4, 8, 16)
POOL_STATE = max(POOL_WINDOWS) - 1
assert all(w & (w - 1) == 0 for w in POOL_WINDOWS)

LANE_TILE = 256
SUBLANES = 8
PROMPT_ROWS = 1024
TILE_ROWS = 512
VMEM_LIMIT_BYTES = 60 * 1024 * 1024

_BF16 = jnp.bfloat16
_F32 = jnp.float32


def _rms(x, g):
    return x * lax.rsqrt(jnp.mean(x * x, axis=-1, keepdims=True) + EPS) * g


def _dot(a, b):
    return jnp.dot(a, b, preferred_element_type=_F32)


def _layer_spec(arr, layer):
    if layer is None:
        blk, idx = arr.shape, (0,) * arr.ndim
    else:
        blk, idx = (None,) + arr.shape[1:], (layer,) + (0,) * (arr.ndim - 1)
    return pl.BlockSpec(blk, lambda *_: idx, pipeline_mode=pl.Buffered(1))


def _big_weights(i):
    kind, j = i % 3, i // 3
    mixer = {0: ("a_w_in", "a_w_out"), 1: ("b_w_in", "b_w_out"), 2: ("c_w_group",)}[kind]
    return [(name, j) for name in mixer] + [("f_w_up", i), ("f_w_down", i)]


def _cast_rows(n_rows, n_steps):
    bf16_rows = 2 * SUBLANES
    if n_rows % (n_steps * bf16_rows) == 0:
        return n_rows // n_steps
    return CHUNK


def _delayed(cur, hist_ref, cols, k, *, step, prompt):
    rows = cur.shape[0]
    if prompt:
        rolled = pltpu.roll(cur, k, 0)
        row = lax.broadcasted_iota(jnp.int32, (SUBLANES, cur.shape[1]), 0)
        top = jnp.where(row < k, pltpu.roll(hist_ref[:, cols], k, 0), rolled[0:SUBLANES, :])
        return jnp.concatenate([top, rolled[SUBLANES:, :]], axis=0)
    past = [h[:, cols] for h in hist_ref[len(hist_ref) - k:]]
    return jnp.concatenate(past + [cur[0:rows - k * step, :]], axis=0)


def _conv3(cur, hist_ref, cw_ref, cols, *, step, prompt):
    return (cw_ref[0:1, cols] * _delayed(cur, hist_ref, cols, 2, step=step, prompt=prompt)
            + cw_ref[1:2, cols] * _delayed(cur, hist_ref, cols, 1, step=step, prompt=prompt)
            + cw_ref[2:3, cols] * cur)


def _save_tail(cur, hist_ref, st_ref, cols, *, step, prompt):
    rows = cur.shape[0]
    st_ref[:, cols] = cur[rows - (CONV_W - 1) * step:rows, :]
    if prompt:
        hist_ref[:, cols] = cur[rows - SUBLANES:rows, :]


def _sconv_part(x_ref, o_ref, hist_ref, st_ref, gn_ref, win_ref, cw_ref, wout_ref,
                h_s, z_s, *, step, prompt):
    d = x_ref.shape[1]
    h_s[...] = _rms(x_ref[...], gn_ref[...]).astype(_BF16)
    for c in range(d // LANE_TILE):
        cols = slice(c * LANE_TILE, (c + 1) * LANE_TILE)
        bg = _dot(h_s[...], win_ref[:, cols])
        cg = _dot(h_s[...], win_ref[:, d + c * LANE_TILE:d + (c + 1) * LANE_TILE])
        v = _dot(h_s[...], win_ref[:, 2 * d + c * LANE_TILE:2 * d + (c + 1) * LANE_TILE])
        u = cg * v
        y = _conv3(u, hist_ref, cw_ref, cols, step=step, prompt=prompt)
        z_s[:, cols] = (bg * y).astype(_BF16)
        _save_tail(u, hist_ref, st_ref, cols, step=step, prompt=prompt)
    o_ref[...] = x_ref[...] + _dot(z_s[...], wout_ref[...])


def _cmlp_prompt_part(x_ref, o_ref, gn_ref, win_ref, gv_ref, ws_ref, bias_ref, wout_ref,
                      h_s, z_s, vb_s, *, rows):
    d = x_ref.shape[1]
    n_heads = ws_ref.shape[0]
    head_dim = d // n_heads
    h_s[...] = _rms(x_ref[...], gn_ref[...]).astype(_BF16)
    vb_s[...] = _rms(_dot(h_s[...], win_ref[:, d:2 * d]), gv_ref[...]).astype(_BF16)

    t_idx = lax.broadcasted_iota(jnp.int32, (CHUNK, CHUNK), 0)
    s_idx = lax.broadcasted_iota(jnp.int32, (CHUNK, CHUNK), 1)
    causal = t_idx >= s_idx
    for hd in range(n_heads):
        cols = slice(hd * head_dim, (hd + 1) * head_dim)
        u = _dot(h_s[...], win_ref[:, cols])
        w_m = jnp.where(causal, ws_ref[hd], 0.0).astype(_BF16)
        bias = bias_ref[:, hd:hd + 1]
        for n in range(rows // CHUNK):
            rws = slice(n * CHUNK, (n + 1) * CHUNK)
            s = _dot(w_m, vb_s[rws, cols]) + bias
            z_s[rws, cols] = (u[rws, :] * s).astype(_BF16)
    o_ref[...] = x_ref[...] + _dot(z_s[...], wout_ref[...])


def _cmlp_sample_part(x_ref, o_ref, v_ref, wsm_ref, bsm_ref, gn_ref, win_ref, gv_ref, wout_ref,
                      h_s, z_s, *, step, rows):
    d = x_ref.shape[1]
    n_steps = rows // step
    n_heads = bsm_ref.shape[0] // n_steps
    head_dim = d // n_heads
    h_s[...] = _rms(x_ref[...], gn_ref[...]).astype(_BF16)
    v_ref[...] = _rms(_dot(h_s[...], win_ref[:, d:2 * d]), gv_ref[...])
    for hd in range(n_heads):
        cols = slice(hd * head_dim, (hd + 1) * head_dim)
        u = _dot(h_s[...], win_ref[:, cols])
        for t in range(n_steps):
            rws = slice(t * step, (t + 1) * step)
            s = jnp.full((step, head_dim), bsm_ref[hd * n_steps + t], _F32)
            for k in range(t + 1):
                w = wsm_ref[(hd * n_steps + t) * n_steps + k]
                s = s + w * v_ref[k * step:(k + 1) * step, cols]
            z_s[rws, cols] = (u[rws, :] * s).astype(_BF16)
    o_ref[...] = x_ref[...] + _dot(z_s[...], wout_ref[...])


def _pool_prompt_part(x_ref, o_ref, st_ref, gn_ref, wg_ref, sc_ref, hbuf, *, rows):
    hist = 2 * SUBLANES
    gdim = wg_ref.shape[1]
    j = pl.program_id(1)

    @pl.when(j == 0)
    def _():
        hbuf[0:hist, :] = jnp.zeros((hist, hbuf.shape[1]), _F32)

    @pl.when(j > 0)
    def _():
        hbuf[0:hist, :] = hbuf[rows:rows + hist, :]

    hbuf[hist:hist + rows, :] = _rms(x_ref[...], gn_ref[...])
    pos = j * rows + lax.broadcasted_iota(jnp.int32, (hist, 1), 0)
    for gi, w in enumerate(POOL_WINDOWS):
        cols = slice(gi * gdim, (gi + 1) * gdim)
        tot = hbuf[:, cols]
        span = 1
        while span < w:
            tot = tot + pltpu.roll(tot, span, 0)
            span *= 2
        tot = tot[hist:, :]
        h = hbuf[hist:hist + rows, cols]
        cnt = jnp.minimum(pos + 1, w).astype(_F32)
        mean = jnp.concatenate([tot[0:hist, :] / cnt, tot[hist:, :] * (1.0 / w)], axis=0)
        dlt = (mean - h).astype(_BF16)
        o_ref[:, cols] = x_ref[:, cols] + _dot(dlt, wg_ref[cols, :]) * sc_ref[:, cols]
    st_ref[...] = hbuf[hist + rows - POOL_STATE:hist + rows, :]


def _pool_sample_part(x_ref, o_ref, state_ref, st_ref, gn_ref, wg_ref, sc_ref, hbuf, *, step, rows):
    n_state = len(state_ref)
    n_new = rows // step
    gdim = wg_ref.shape[1]
    hbuf[...] = _rms(x_ref[...], gn_ref[...])
    for gi, w in enumerate(POOL_WINDOWS):
        cols = slice(gi * gdim, (gi + 1) * gdim)
        h = hbuf[:, cols]
        tot = h
        for k in range(1, w):
            past = [s[:, cols] for s in state_ref[n_state - k:n_state - k + n_new]]
            if k < n_new:
                past.append(hbuf[0:rows - k * step, cols])
            tot = tot + jnp.concatenate(past, axis=0)
        dlt = (tot * (1.0 / w) - h).astype(_BF16)
        o_ref[:, cols] = x_ref[:, cols] + _dot(dlt, wg_ref[cols, :]) * sc_ref[:, cols]
    for t in range(n_state - n_new):
        st_ref[t * step:(t + 1) * step, :] = state_ref[n_new + t][...]
    st_ref[(n_state - n_new) * step:n_state * step, :] = hbuf[...]


def _ffn_part(o_ref, hist_ref, st_ref, gn_ref, wup_ref, cw_ref, cb_ref, wdn_ref, gfin_ref,
              h_s, act_s, *, step, prompt):
    d_ff = wdn_ref.shape[0]
    h_s[...] = _rms(o_ref[...], gn_ref[...]).astype(_BF16)
    for c in range(d_ff // LANE_TILE):
        cols = slice(c * LANE_TILE, (c + 1) * LANE_TILE)
        g = _dot(h_s[...], wup_ref[:, cols])
        a = _dot(h_s[...], wup_ref[:, d_ff + c * LANE_TILE:d_ff + (c + 1) * LANE_TILE])
        y = _conv3(g, hist_ref, cw_ref, cols, step=step, prompt=prompt) + cb_ref[:, cols]
        half = 0.5 * y
        act_s[:, cols] = (half * (1.0 + jnp.tanh(half)) * a).astype(_BF16)
        _save_tail(g, hist_ref, st_ref, cols, step=step, prompt=prompt)
    out = o_ref[...] + _dot(act_s[...], wdn_ref[...])
    if gfin_ref is not None:
        out = _rms(out, gfin_ref[...])
    o_ref[...] = out


def _layer_kernel(*refs, kind, prompt, final_norm, step, rows, n_cast, n_grid):
    it = iter(refs)
    nxt = lambda n=1: next(it) if n == 1 else [next(it) for _ in range(n)]
    conv_kw = dict(step=step, prompt=prompt)
    if kind == 1 and not prompt:
        wsm_ref, bsm_ref = nxt(2)
    x_ref = nxt()
    if not prompt:
        n_mix_state = {0: CONV_W - 1, 1: 0, 2: POOL_STATE}[kind]
        mix_state = [nxt() for _ in range(n_mix_state)]
        ffn_state = [nxt() for _ in range(CONV_W - 1)]
    n_mix_w = {0: 4, 1: 6 if prompt else 4, 2: 3}[kind]
    mix_w = nxt(n_mix_w)
    gffn_ref, wup_hbm, fcw_ref, fcb_ref, wdn_hbm = nxt(5)
    gfin_ref = nxt() if final_norm else None
    cast_in = [nxt() for _ in range(n_cast)]
    o_ref = nxt()
    mix_out = nxt() if not (prompt and kind == 1) else None
    ffn_st = nxt()
    cast_out = [nxt() for _ in range(n_cast)]
    h_s, act_s, wup_v, wdn_v, w_sem = nxt(5)

    ffn_w = (gffn_ref, wup_v, fcw_ref, fcb_ref, wdn_v)
    ffn_copies = (pltpu.make_async_copy(wup_hbm, wup_v, w_sem.at[0]),
                  pltpu.make_async_copy(wdn_hbm, wdn_v, w_sem.at[1]))
    first_step = functools.reduce(jnp.logical_and, [pl.program_id(a) == 0 for a in range(n_grid)])

    @pl.when(first_step)
    def _():
        for cp in ffn_copies:
            cp.start()

    for src, dst in zip(cast_in, cast_out):
        dst[...] = src[...].astype(_BF16)
    if prompt:
        ffn_carry = nxt()

        @pl.when(pl.program_id(1) == 0)
        def _():
            ffn_carry[...] = jnp.zeros(ffn_carry.shape, _F32)
    else:
        ffn_carry = ffn_state

    if kind == 0:
        z_s = nxt()
        if prompt:
            mix_carry = nxt()

            @pl.when(pl.program_id(1) == 0)
            def _():
                mix_carry[...] = jnp.zeros(mix_carry.shape, _F32)
        else:
            mix_carry = mix_state
        _sconv_part(x_ref, o_ref, mix_carry, mix_out, *mix_w, h_s, z_s, **conv_kw)
    elif kind == 1:
        if prompt:
            z_s, vb_s = nxt(2)
            _cmlp_prompt_part(x_ref, o_ref, *mix_w, h_s, z_s, vb_s, rows=rows)
        else:
            z_s = nxt()
            _cmlp_sample_part(x_ref, o_ref, mix_out, wsm_ref, bsm_ref, *mix_w, h_s, z_s,
                              step=step, rows=rows)
    else:
        hbuf = nxt()
        if prompt:
            _pool_prompt_part(x_ref, o_ref, mix_out, *mix_w, hbuf, rows=rows)
        else:
            _pool_sample_part(x_ref, o_ref, mix_state, mix_out, *mix_w, hbuf, step=step, rows=rows)

    @pl.when(first_step)
    def _():
        for cp in ffn_copies:
            cp.wait()

    _ffn_part(o_ref, ffn_carry, ffn_st, *ffn_w, gfin_ref, h_s, act_s, **conv_kw)


def _layer(x, i, mix_state, ffn_state, w, wb, cast_next, *, prompt, group):
    n, d = x.shape
    d_ff = wb["f_w_down"].shape[0]
    depth = w["g_mix"].shape[0]
    kind, j = i % 3, i // 3
    final_norm = i == depth - 1
    rows = PROMPT_ROWS if prompt else TILE_ROWS
    if prompt:
        n_t = n // group // rows
        grid = (group, n_t)
        step = 1
        x_spec = pl.BlockSpec((rows, d), lambda b, t: (b * n_t + t, 0))

        def state_spec(n_rows, width):
            return pl.BlockSpec((None, n_rows, width), lambda b, t: (b, 0, 0))

        def state_shape(n_rows, width):
            return jax.ShapeDtypeStruct((group, n_rows, width), _F32)
    else:
        grid = (n // rows,)
        step = group
        x_spec = pl.BlockSpec((rows, d), lambda g: (g, 0))

        def state_spec(n_rows, width):
            return pl.BlockSpec((n_rows * step, width), lambda g: (g, 0))

        def state_shape(n_rows, width):
            return jax.ShapeDtypeStruct((grid[0] * n_rows * step, width), _F32)

        def add_steps(arr, layer):
            n_k = arr.shape[1] // (grid[0] * group)
            for k in range(n_k):
                add_in(arr, pl.BlockSpec((None, group, arr.shape[2]),
                                         lambda g, k=k: (layer, g * n_k + k, 0)))

    ins, in_specs, outs, out_specs, scratch = [], [], [], [], []

    def add_in(arr, spec):
        ins.append(arr)
        in_specs.append(spec)

    def add_w(name, layer):
        if name in wb:
            add_in(wb[name], _layer_spec(wb[name], None))
        else:
            add_in(w[name], _layer_spec(w[name], layer))

    if kind == 1 and not prompt:
        n_steps = rows // step
        smem = pl.BlockSpec(memory_space=pltpu.SMEM)
        add_in(w["b_w_s"][j, :, :n_steps, :n_steps].reshape(-1), smem)
        add_in(w["b_bias"][j, :, :n_steps].reshape(-1), smem)
    add_in(x, x_spec)
    if not prompt:
        if kind != 1:
            add_steps(mix_state, j)
        add_steps(ffn_state, i)
    add_w("g_mix", i)
    if kind == 0:
        for name in ("a_w_in", "a_conv", "a_w_out"):
            add_w(name, j)
    elif kind == 1:
        names = ("b_w_in", "b_g_v", "b_w_s", "b_bias_t", "b_w_out") if prompt else (
            "b_w_in", "b_g_v", "b_w_out")
        for name in names:
            add_w(name, j)
    else:
        for name in ("c_w_group", "c_scale"):
            add_w(name, j)
    for name in ("g_ffn", "f_w_up", "f_conv", "f_conv_b", "f_w_down"):
        if name in wb:
            add_in(wb[name], pl.BlockSpec(memory_space=pl.ANY))
        else:
            add_w(name, i)
    if final_norm:
        add_in(w["g_final"], pl.BlockSpec(w["g_final"].shape, lambda *_: (0, 0)))
    cast_specs = []
    for name, stack, idx in cast_next:
        n_rows, width = stack.shape[1:]
        blk_rows = _cast_rows(n_rows, n // rows)
        last = n_rows // blk_rows - 1
        add_in(stack, pl.BlockSpec(
            (None, blk_rows, width),
            lambda b, t, idx=idx, last=last: (idx, jnp.minimum(b * n_t + t, last), 0)))
        cast_specs.append(pl.BlockSpec(
            (blk_rows, width), lambda b, t, last=last: (jnp.minimum(b * n_t + t, last), 0)))

    outs.append(jax.ShapeDtypeStruct((n, d), _F32))
    out_specs.append(x_spec)
    if kind == 0:
        outs.append(state_shape(CONV_W - 1, d))
        out_specs.append(state_spec(CONV_W - 1, d))
    elif kind == 1 and not prompt:
        outs.append(jax.ShapeDtypeStruct((n, d), _F32))
        out_specs.append(x_spec)
    elif kind == 2:
        outs.append(state_shape(POOL_STATE, d))
        out_specs.append(state_spec(POOL_STATE, d))
    outs.append(state_shape(CONV_W - 1, d_ff))
    out_specs.append(state_spec(CONV_W - 1, d_ff))
    n_fixed_outs = len(outs)
    for (name, stack, idx), spec in zip(cast_next, cast_specs):
        outs.append(jax.ShapeDtypeStruct(stack.shape[1:], _BF16))
        out_specs.append(spec)

    scratch += [pltpu.VMEM((rows, d), _BF16), pltpu.VMEM((rows, d_ff), _BF16),
                pltpu.VMEM(wb["f_w_up"].shape, _BF16), pltpu.VMEM(wb["f_w_down"].shape, _BF16),
                pltpu.SemaphoreType.DMA((2,))]
    if prompt:
        scratch.append(pltpu.VMEM((SUBLANES, d_ff), _F32))
    if kind == 0:
        scratch.append(pltpu.VMEM((rows, d), _BF16))
        if prompt:
            scratch.append(pltpu.VMEM((SUBLANES, d), _F32))
    elif kind == 1:
        scratch.append(pltpu.VMEM((rows, d), _BF16))
        if prompt:
            scratch.append(pltpu.VMEM((rows, d), _BF16))
    else:
        scratch.append(pltpu.VMEM((2 * SUBLANES + rows if prompt else rows, d), _F32))

    res = pl.pallas_call(
        functools.partial(_layer_kernel, kind=kind, prompt=prompt, final_norm=final_norm,
                          step=step, rows=rows, n_cast=len(cast_next), n_grid=len(grid)),
        grid=grid,
        in_specs=in_specs,
        out_specs=out_specs,
        out_shape=outs,
        scratch_shapes=scratch,
        compiler_params=pltpu.CompilerParams(
            dimension_semantics=("arbitrary",) * len(grid),
            vmem_limit_bytes=VMEM_LIMIT_BYTES),
        name=f"layer{i}_{'prompt' if prompt else 'sample'}",
    )(*ins)
    x_new, ffn_new = res[0], res[n_fixed_outs - 1]
    mix_new = res[1] if n_fixed_outs == 3 else None
    wb_next = {name: arr for (name, _, _), arr in zip(cast_next, res[n_fixed_outs:])}
    return x_new, mix_new, ffn_new, wb_next


def _trunk(x, conv_prev, pool_prev, ffn_prev, w, wbs, big, *, prompt, group):
    depth = w["g_mix"].shape[0]
    new = {0: [], 1: [], 2: []}
    new_ffn = []
    for i in range(depth):
        kind = i % 3
        mix_state = conv_prev if kind == 0 else pool_prev
        cast_next = []
        if prompt and i + 1 < depth:
            cast_next = [(name, big[name], idx) for name, idx in _big_weights(i + 1)]
        x, mix_new, ffn_new, wb_next = _layer(x, i, mix_state, ffn_prev, w, wbs[i], cast_next,
                                              prompt=prompt, group=group)
        if cast_next:
            wbs.append(wb_next)
        if mix_new is not None:
            new[kind].append(mix_new)
        new_ffn.append(ffn_new)
    return x, jnp.stack(new[0]), jnp.stack(new[2]), jnp.stack(new_ffn), new[1]


def _cast_kernel(*refs):
    n = len(refs) // 2
    for src, dst in zip(refs[:n], refs[n:]):
        dst[...] = src[...].astype(_BF16)


def _cast_layer0(big):
    n_steps = 8
    ins, in_specs, outs, out_specs = [], [], [], []
    for name, idx in _big_weights(0):
        stack = big[name]
        n_rows, width = stack.shape[1:]
        blk_rows = _cast_rows(n_rows, n_steps)
        last = n_rows // blk_rows - 1
        ins.append(stack)
        in_specs.append(pl.BlockSpec((None, blk_rows, width),
                                     lambda s, idx=idx, last=last: (idx, jnp.minimum(s, last), 0)))
        outs.append(jax.ShapeDtypeStruct((n_rows, width), _BF16))
        out_specs.append(pl.BlockSpec((blk_rows, width),
                                      lambda s, last=last: (jnp.minimum(s, last), 0)))
    res = pl.pallas_call(
        _cast_kernel, grid=(n_steps,), in_specs=in_specs, out_specs=out_specs, out_shape=outs,
        compiler_params=pltpu.CompilerParams(dimension_semantics=("arbitrary",),
                                             vmem_limit_bytes=VMEM_LIMIT_BYTES),
        name="cast_layer0",
    )(*ins)
    return {name: arr for (name, _), arr in zip(_big_weights(0), res)}


def _to_groups(a, group):
    *lead, b, k, c = a.shape
    a = a.reshape(*lead, b // group, group, k, c)
    a = jnp.swapaxes(a, -3, -2)
    return a.reshape(*lead, b * k, c)


def _from_groups(a, group, k):
    *lead, n, c = a.shape
    a = a.reshape(*lead, n // (k * group), k, group, c)
    a = jnp.swapaxes(a, -3, -2)
    return a.reshape(*lead, n // k, k, c)


def kernel(x_prompt, x_sample, state_shortconv, state_pool, state_ffnconv, g_mix, g_ffn, g_final, a_w_in, a_conv, a_w_out, b_w_in, b_g_v, b_w_s, b_bias, b_w_out, c_w_group, c_scale, f_w_up, f_conv, f_conv_b, f_w_down):
    batch, seq, d = x_prompt.shape
    dec_batch, dec_seq, _ = x_sample.shape
    row = lambda a: a[:, None, :]
    w = dict(g_mix=row(g_mix), g_ffn=row(g_ffn), g_final=g_final[None, :],
             a_conv=a_conv, b_g_v=row(b_g_v), b_w_s=b_w_s, b_bias=b_bias,
             b_bias_t=jnp.swapaxes(b_bias, 1, 2), c_scale=row(c_scale),
             f_conv=f_conv, f_conv_b=row(f_conv_b))
    big = dict(a_w_in=a_w_in, a_w_out=a_w_out, b_w_in=b_w_in, b_w_out=b_w_out,
               c_w_group=c_w_group.reshape(c_w_group.shape[0], -1, c_w_group.shape[-1]),
               f_w_up=f_w_up, f_w_down=f_w_down)
    wbs = [_cast_layer0(big)]

    yp, conv_p, pool_p, ffn_p, _ = _trunk(
        x_prompt.reshape(batch * seq, d), None, None, None, w, wbs, big, prompt=True, group=batch)

    group = TILE_ROWS // dec_seq
    tg = functools.partial(_to_groups, group=group)
    ys, conv_s, pool_s, ffn_s, v_s = _trunk(
        tg(x_sample), tg(state_shortconv), tg(state_pool), tg(state_ffnconv), w, wbs, big,
        prompt=False, group=group)
    fg = functools.partial(_from_groups, group=group)

    return (yp.reshape(batch, seq, d),
            fg(ys, k=dec_seq),
            conv_p, fg(conv_s, k=CONV_W - 1),
            pool_p, fg(pool_s, k=POOL_STATE),
            ffn_p, fg(ffn_s, k=CONV_W - 1),
            fg(jnp.stack(v_s), k=dec_seq))
```

```python
import functools

import jax
import jax.numpy as jnp
from jax import lax
from jax.experimental import pallas as pl
from jax.experimental.pallas import tpu as pltpu

EPS = 1e-6
CONV_W = 3
CHUNK = 128
POOL_WINDOWS = (2, 4, 8, 16)
POOL_STATE = max(POOL_WINDOWS) - 1
assert all(w & (w - 1) == 0 for w in POOL_WINDOWS)

LANE_TILE = 256
SUBLANES = 8
PROMPT_ROWS = 1024
TILE_ROWS = 512
VMEM_LIMIT_BYTES = 60 * 1024 * 1024

_BF16 = jnp.bfloat16
_F32 = jnp.float32


def _rms(x, g):
    return x * lax.rsqrt(jnp.mean(x * x, axis=-1, keepdims=True) + EPS) * g


def _dot(a, b):
    return jnp.dot(a, b, preferred_element_type=_F32)


def _layer_spec(arr, layer):
    if layer is None:
        blk, idx = arr.shape, (0,) * arr.ndim
    else:
        blk, idx = (None,) + arr.shape[1:], (layer,) + (0,) * (arr.ndim - 1)
    return pl.BlockSpec(blk, lambda *_: idx, pipeline_mode=pl.Buffered(1))


def _big_weights(i):
    kind, j = i % 3, i // 3
    mixer = {0: ("a_w_in", "a_w_out"), 1: ("b_w_in", "b_w_out"), 2: ("c_w_group",)}[kind]
    return [(name, j) for name in mixer] + [("f_w_up", i), ("f_w_down", i)]


def _cast_rows(n_rows, n_steps):
    bf16_rows = 2 * SUBLANES
    if n_rows % (n_steps * bf16_rows) == 0:
        return n_rows // n_steps
    return CHUNK


def _delayed(cur, hist_ref, cols, k, *, step, prompt):
    rows = cur.shape[0]
    if prompt:
        rolled = pltpu.roll(cur, k, 0)
        row = lax.broadcasted_iota(jnp.int32, (SUBLANES, cur.shape[1]), 0)
        top = jnp.where(row < k, pltpu.roll(hist_ref[:, cols], k, 0), rolled[0:SUBLANES, :])
        return jnp.concatenate([top, rolled[SUBLANES:, :]], axis=0)
    past = [h[:, cols] for h in hist_ref[len(hist_ref) - k:]]
    return jnp.concatenate(past + [cur[0:rows - k * step, :]], axis=0)


def _conv3(cur, hist_ref, cw_ref, cols, *, step, prompt):
    return (cw_ref[0:1, cols] * _delayed(cur, hist_ref, cols, 2, step=step, prompt=prompt)
            + cw_ref[1:2, cols] * _delayed(cur, hist_ref, cols, 1, step=step, prompt=prompt)
            + cw_ref[2:3, cols] * cur)


def _save_tail(cur, hist_ref, st_ref, cols, *, step, prompt):
    rows = cur.shape[0]
    st_ref[:, cols] = cur[rows - (CONV_W - 1) * step:rows, :]
    if prompt:
        hist_ref[:, cols] = cur[rows - SUBLANES:rows, :]


def _sconv_part(x_ref, o_ref, hist_ref, st_ref, gn_ref, win_ref, cw_ref, wout_ref,
                h_s, z_s, *, step, prompt):
    d = x_ref.shape[1]
    h_s[...] = _rms(x_ref[...], gn_ref[...]).astype(_BF16)
    for c in range(d // LANE_TILE):
        cols = slice(c * LANE_TILE, (c + 1) * LANE_TILE)
        bg = _dot(h_s[...], win_ref[:, cols])
        cg = _dot(h_s[...], win_ref[:, d + c * LANE_TILE:d + (c + 1) * LANE_TILE])
        v = _dot(h_s[...], win_ref[:, 2 * d + c * LANE_TILE:2 * d + (c + 1) * LANE_TILE])
        u = cg * v
        y = _conv3(u, hist_ref, cw_ref, cols, step=step, prompt=prompt)
        z_s[:, cols] = (bg * y).astype(_BF16)
        _save_tail(u, hist_ref, st_ref, cols, step=step, prompt=prompt)
    o_ref[...] = x_ref[...] + _dot(z_s[...], wout_ref[...])


def _cmlp_prompt_part(x_ref, o_ref, gn_ref, win_ref, gv_ref, ws_ref, bias_ref, wout_ref,
                      h_s, z_s, vb_s, *, rows):
    d = x_ref.shape[1]
    n_heads = ws_ref.shape[0]
    head_dim = d // n_heads
    h_s[...] = _rms(x_ref[...], gn_ref[...]).astype(_BF16)
    vb_s[...] = _rms(_dot(h_s[...], win_ref[:, d:2 * d]), gv_ref[...]).astype(_BF16)

    t_idx = lax.broadcasted_iota(jnp.int32, (CHUNK, CHUNK), 0)
    s_idx = lax.broadcasted_iota(jnp.int32, (CHUNK, CHUNK), 1)
    causal = t_idx >= s_idx
    for hd in range(n_heads):
        cols = slice(hd * head_dim, (hd + 1) * head_dim)
        u = _dot(h_s[...], win_ref[:, cols])
        w_m = jnp.where(causal, ws_ref[hd], 0.0).astype(_BF16)
        bias = bias_ref[:, hd:hd + 1]
        for n in range(rows // CHUNK):
            rws = slice(n * CHUNK, (n + 1) * CHUNK)
            s = _dot(w_m, vb_s[rws, cols]) + bias
            z_s[rws, cols] = (u[rws, :] * s).astype(_BF16)
    o_ref[...] = x_ref[...] + _dot(z_s[...], wout_ref[...])


def _cmlp_sample_part(x_ref, o_ref, v_ref, wsm_ref, bsm_ref, gn_ref, win_ref, gv_ref, wout_ref,
                      h_s, z_s, *, step, rows):
    d = x_ref.shape[1]
    n_steps = rows // step
    n_heads = bsm_ref.shape[0] // n_steps
    head_dim = d // n_heads
    h_s[...] = _rms(x_ref[...], gn_ref[...]).astype(_BF16)
    v_ref[...] = _rms(_dot(h_s[...], win_ref[:, d:2 * d]), gv_ref[...])
    for hd in range(n_heads):
        cols = slice(hd * head_dim, (hd + 1) * head_dim)
        u = _dot(h_s[...], win_ref[:, cols])
        for t in range(n_steps):
            rws = slice(t * step, (t + 1) * step)
            s = jnp.full((step, head_dim), bsm_ref[hd * n_steps + t], _F32)
            for k in range(t + 1):
                w = wsm_ref[(hd * n_steps + t) * n_steps + k]
                s = s + w * v_ref[k * step:(k + 1) * step, cols]
            z_s[rws, cols] = (u[rws, :] * s).astype(_BF16)
    o_ref[...] = x_ref[...] + _dot(z_s[...], wout_ref[...])


def _pool_prompt_part(x_ref, o_ref, st_ref, gn_ref, wg_ref, sc_ref, hbuf, *, rows):
    hist = 2 * SUBLANES
    gdim = wg_ref.shape[1]
    j = pl.program_id(1)

    @pl.when(j == 0)
    def _():
        hbuf[0:hist, :] = jnp.zeros((hist, hbuf.shape[1]), _F32)

    @pl.when(j > 0)
    def _():
        hbuf[0:hist, :] = hbuf[rows:rows + hist, :]

    hbuf[hist:hist + rows, :] = _rms(x_ref[...], gn_ref[...])
    pos = j * rows + lax.broadcasted_iota(jnp.int32, (hist, 1), 0)
    for gi, w in enumerate(POOL_WINDOWS):
        cols = slice(gi * gdim, (gi + 1) * gdim)
        tot = hbuf[:, cols]
        span = 1
        while span < w:
            tot = tot + pltpu.roll(tot, span, 0)
            span *= 2
        tot = tot[hist:, :]
        h = hbuf[hist:hist + rows, cols]
        cnt = jnp.minimum(pos + 1, w).astype(_F32)
        mean = jnp.concatenate([tot[0:hist, :] / cnt, tot[hist:, :] * (1.0 / w)], axis=0)
        dlt = (mean - h).astype(_BF16)
        o_ref[:, cols] = x_ref[:, cols] + _dot(dlt, wg_ref[cols, :]) * sc_ref[:, cols]
    st_ref[...] = hbuf[hist + rows - POOL_STATE:hist + rows, :]


def _pool_sample_part(x_ref, o_ref, state_ref, st_ref, gn_ref, wg_ref, sc_ref, hbuf, *, step, rows):
    n_state = len(state_ref)
    n_new = rows // step
    gdim = wg_ref.shape[1]
    hbuf[...] = _rms(x_ref[...], gn_ref[...])
    for gi, w in enumerate(POOL_WINDOWS):
        cols = slice(gi * gdim, (gi + 1) * gdim)
        h = hbuf[:, cols]
        tot = h
        for k in range(1, w):
            past = [s[:, cols] for s in state_ref[n_state - k:n_state - k + n_new]]
            if k < n_new:
                past.append(hbuf[0:rows - k * step, cols])
            tot = tot + jnp.concatenate(past, axis=0)
        dlt = (tot * (1.0 / w) - h).astype(_BF16)
        o_ref[:, cols] = x_ref[:, cols] + _dot(dlt, wg_ref[cols, :]) * sc_ref[:, cols]
    for t in range(n_state - n_new):
        st_ref[t * step:(t + 1) * step, :] = state_ref[n_new + t][...]
    st_ref[(n_state - n_new) * step:n_state * step, :] = hbuf[...]


def _ffn_part(o_ref, hist_ref, st_ref, gn_ref, wup_ref, cw_ref, cb_ref, wdn_ref, gfin_ref,
              h_s, act_s, *, step, prompt):
    d_ff = wdn_ref.shape[0]
    h_s[...] = _rms(o_ref[...], gn_ref[...]).astype(_BF16)
    for c in range(d_ff // LANE_TILE):
        cols = slice(c * LANE_TILE, (c + 1) * LANE_TILE)
        g = _dot(h_s[...], wup_ref[:, cols])
        a = _dot(h_s[...], wup_ref[:, d_ff + c * LANE_TILE:d_ff + (c + 1) * LANE_TILE])
        y = _conv3(g, hist_ref, cw_ref, cols, step=step, prompt=prompt) + cb_ref[:, cols]
        half = 0.5 * y
        act_s[:, cols] = (half * (1.0 + jnp.tanh(half)) * a).astype(_BF16)
        _save_tail(g, hist_ref, st_ref, cols, step=step, prompt=prompt)
    out = o_ref[...] + _dot(act_s[...], wdn_ref[...])
    if gfin_ref is not None:
        out = _rms(out, gfin_ref[...])
    o_ref[...] = out


def _layer_kernel(*refs, kind, prompt, final_norm, step, rows, n_cast, n_grid):
    it = iter(refs)
    nxt = lambda n=1: next(it) if n == 1 else [next(it) for _ in range(n)]
    conv_kw = dict(step=step, prompt=prompt)
    if kind == 1 and not prompt:
        wsm_ref, bsm_ref = nxt(2)
    x_ref = nxt()
    if not prompt:
        n_mix_state = {0: CONV_W - 1, 1: 0, 2: POOL_STATE}[kind]
        mix_state = [nxt() for _ in range(n_mix_state)]
        ffn_state = [nxt() for _ in range(CONV_W - 1)]
    n_mix_w = {0: 4, 1: 6 if prompt else 4, 2: 3}[kind]
    mix_w = nxt(n_mix_w)
    gffn_ref, wup_hbm, fcw_ref, fcb_ref, wdn_hbm = nxt(5)
    gfin_ref = nxt() if final_norm else None
    cast_in = [nxt() for _ in range(n_cast)]
    o_ref = nxt()
    mix_out = nxt() if not (prompt and kind == 1) else None
    ffn_st = nxt()
    cast_out = [nxt() for _ in range(n_cast)]
    h_s, act_s, wup_v, wdn_v, w_sem = nxt(5)

    ffn_w = (gffn_ref, wup_v, fcw_ref, fcb_ref, wdn_v)
    ffn_copies = (pltpu.make_async_copy(wup_hbm, wup_v, w_sem.at[0]),
                  pltpu.make_async_copy(wdn_hbm, wdn_v, w_sem.at[1]))
    first_step = functools.reduce(jnp.logical_and, [pl.program_id(a) == 0 for a in range(n_grid)])

    @pl.when(first_step)
    def _():
        for cp in ffn_copies:
            cp.start()

    for src, dst in zip(cast_in, cast_out):
        dst[...] = src[...].astype(_BF16)
    if prompt:
        ffn_carry = nxt()

        @pl.when(pl.program_id(1) == 0)
        def _():
            ffn_carry[...] = jnp.zeros(ffn_carry.shape, _F32)
    else:
        ffn_carry = ffn_state

    if kind == 0:
        z_s = nxt()
        if prompt:
            mix_carry = nxt()

            @pl.when(pl.program_id(1) == 0)
            def _():
                mix_carry[...] = jnp.zeros(mix_carry.shape, _F32)
        else:
            mix_carry = mix_state
        _sconv_part(x_ref, o_ref, mix_carry, mix_out, *mix_w, h_s, z_s, **conv_kw)
    elif kind == 1:
        if prompt:
            z_s, vb_s = nxt(2)
            _cmlp_prompt_part(x_ref, o_ref, *mix_w, h_s, z_s, vb_s, rows=rows)
        else:
            z_s = nxt()
            _cmlp_sample_part(x_ref, o_ref, mix_out, wsm_ref, bsm_ref, *mix_w, h_s, z_s,
                              step=step, rows=rows)
    else:
        hbuf = nxt()
        if prompt:
            _pool_prompt_part(x_ref, o_ref, mix_out, *mix_w, hbuf, rows=rows)
        else:
            _pool_sample_part(x_ref, o_ref, mix_state, mix_out, *mix_w, hbuf, step=step, rows=rows)

    @pl.when(first_step)
    def _():
        for cp in ffn_copies:
            cp.wait()

    _ffn_part(o_ref, ffn_carry, ffn_st, *ffn_w, gfin_ref, h_s, act_s, **conv_kw)


def _layer(x, i, mix_state, ffn_state, w, wb, cast_next, *, prompt, group):
    n, d = x.shape
    d_ff = wb["f_w_down"].shape[0]
    depth = w["g_mix"].shape[0]
    kind, j = i % 3, i // 3
    final_norm = i == depth - 1
    rows = PROMPT_ROWS if prompt else TILE_ROWS
    if prompt:
        n_t = n // group // rows
        grid = (group, n_t)
        step = 1
        x_spec = pl.BlockSpec((rows, d), lambda b, t: (b * n_t + t, 0))

        def state_spec(n_rows, width):
            return pl.BlockSpec((None, n_rows, width), lambda b, t: (b, 0, 0))

        def state_shape(n_rows, width):
            return jax.ShapeDtypeStruct((group, n_rows, width), _F32)
    else:
        grid = (n // rows,)
        step = group
        x_spec = pl.BlockSpec((rows, d), lambda g: (g, 0))

        def state_spec(n_rows, width):
            return pl.BlockSpec((n_rows * step, width), lambda g: (g, 0))

        def state_shape(n_rows, width):
            return jax.ShapeDtypeStruct((grid[0] * n_rows * step, width), _F32)

        def add_steps(arr, layer):
            n_k = arr.shape[1] // (grid[0] * group)
            for k in range(n_k):
                add_in(arr, pl.BlockSpec((None, group, arr.shape[2]),
                                         lambda g, k=k: (layer, g * n_k + k, 0)))

    ins, in_specs, outs, out_specs, scratch = [], [], [], [], []

    def add_in(arr, spec):
        ins.append(arr)
        in_specs.append(spec)

    def add_w(name, layer):
        if name in wb:
            add_in(wb[name], _layer_spec(wb[name], None))
        else:
            add_in(w[name], _layer_spec(w[name], layer))

    if kind == 1 and not prompt:
        n_steps = rows // step
        smem = pl.BlockSpec(memory_space=pltpu.SMEM)
        add_in(w["b_w_s"][j, :, :n_steps, :n_steps].reshape(-1), smem)
        add_in(w["b_bias"][j, :, :n_steps].reshape(-1), smem)
    x_operand = len(ins)
    add_in(x, x_spec)
    if not prompt:
        if kind != 1:
            add_steps(mix_state, j)
        add_steps(ffn_state, i)
    add_w("g_mix", i)
    if kind == 0:
        for name in ("a_w_in", "a_conv", "a_w_out"):
            add_w(name, j)
    elif kind == 1:
        names = ("b_w_in", "b_g_v", "b_w_s", "b_bias_t", "b_w_out") if prompt else (
            "b_w_in", "b_g_v", "b_w_out")
        for name in names:
            add_w(name, j)
    else:
        for name in ("c_w_group", "c_scale"):
            add_w(name, j)
    for name in ("g_ffn", "f_w_up", "f_conv", "f_conv_b", "f_w_down"):
        if name in wb:
            add_in(wb[name], pl.BlockSpec(memory_space=pl.ANY))
        else:
            add_w(name, i)
    if final_norm:
        add_in(w["g_final"], pl.BlockSpec(w["g_final"].shape, lambda *_: (0, 0)))
    cast_specs = []
    for name, stack, idx in cast_next:
        n_rows, width = stack.shape[1:]
        blk_rows = _cast_rows(n_rows, n // rows)
        last = n_rows // blk_rows - 1
        add_in(stack, pl.BlockSpec(
            (None, blk_rows, width),
            lambda b, t, idx=idx, last=last: (idx, jnp.minimum(b * n_t + t, last), 0)))
        cast_specs.append(pl.BlockSpec(
            (blk_rows, width), lambda b, t, last=last: (jnp.minimum(b * n_t + t, last), 0)))

    outs.append(jax.ShapeDtypeStruct((n, d), _F32))
    out_specs.append(x_spec)
    if kind == 0:
        outs.append(state_shape(CONV_W - 1, d))
        out_specs.append(state_spec(CONV_W - 1, d))
    elif kind == 1 and not prompt:
        outs.append(jax.ShapeDtypeStruct((n, d), _F32))
        out_specs.append(x_spec)
    elif kind == 2:
        outs.append(state_shape(POOL_STATE, d))
        out_specs.append(state_spec(POOL_STATE, d))
    outs.append(state_shape(CONV_W - 1, d_ff))
    out_specs.append(state_spec(CONV_W - 1, d_ff))
    n_fixed_outs = len(outs)
    for (name, stack, idx), spec in zip(cast_next, cast_specs):
        outs.append(jax.ShapeDtypeStruct(stack.shape[1:], _BF16))
        out_specs.append(spec)

    scratch += [pltpu.VMEM((rows, d), _BF16), pltpu.VMEM((rows, d_ff), _BF16),
                pltpu.VMEM(wb["f_w_up"].shape, _BF16), pltpu.VMEM(wb["f_w_down"].shape, _BF16),
                pltpu.SemaphoreType.DMA((2,))]
    if prompt:
        scratch.append(pltpu.VMEM((SUBLANES, d_ff), _F32))
    if kind == 0:
        scratch.append(pltpu.VMEM((rows, d), _BF16))
        if prompt:
            scratch.append(pltpu.VMEM((SUBLANES, d), _F32))
    elif kind == 1:
        scratch.append(pltpu.VMEM((rows, d), _BF16))
        if prompt:
            scratch.append(pltpu.VMEM((rows, d), _BF16))
    else:
        scratch.append(pltpu.VMEM((2 * SUBLANES + rows if prompt else rows, d), _F32))

    res = pl.pallas_call(
        functools.partial(_layer_kernel, kind=kind, prompt=prompt, final_norm=final_norm,
                          step=step, rows=rows, n_cast=len(cast_next), n_grid=len(grid)),
        grid=grid,
        in_specs=in_specs,
        out_specs=out_specs,
        out_shape=outs,
        scratch_shapes=scratch,
        input_output_aliases={x_operand: 0} if i > 0 else {},
        compiler_params=pltpu.CompilerParams(
            dimension_semantics=("arbitrary",) * len(grid),
            vmem_limit_bytes=VMEM_LIMIT_BYTES),
        name=f"layer{i}_{'prompt' if prompt else 'sample'}",
    )(*ins)
    x_new, ffn_new = res[0], res[n_fixed_outs - 1]
    mix_new = res[1] if n_fixed_outs == 3 else None
    wb_next = {name: arr for (name, _, _), arr in zip(cast_next, res[n_fixed_outs:])}
    return x_new, mix_new, ffn_new, wb_next


def _trunk(x, conv_prev, pool_prev, ffn_prev, w, wbs, big, *, prompt, group):
    depth = w["g_mix"].shape[0]
    new = {0: [], 1: [], 2: []}
    new_ffn = []
    for i in range(depth):
        kind = i % 3
        mix_state = conv_prev if kind == 0 else pool_prev
        cast_next = []
        if prompt and i + 1 < depth:
            cast_next = [(name, big[name], idx) for name, idx in _big_weights(i + 1)]
        x, mix_new, ffn_new, wb_next = _layer(x, i, mix_state, ffn_prev, w, wbs[i], cast_next,
                                              prompt=prompt, group=group)
        if cast_next:
            wbs.append(wb_next)
        if mix_new is not None:
            new[kind].append(mix_new)
        new_ffn.append(ffn_new)
    return x, jnp.stack(new[0]), jnp.stack(new[2]), jnp.stack(new_ffn), new[1]


def _cast_kernel(*refs):
    n = len(refs) // 2
    for src, dst in zip(refs[:n], refs[n:]):
        dst[...] = src[...].astype(_BF16)


def _cast_layer0(big):
    n_steps = 8
    ins, in_specs, outs, out_specs = [], [], [], []
    for name, idx in _big_weights(0):
        stack = big[name]
        n_rows, width = stack.shape[1:]
        blk_rows = _cast_rows(n_rows, n_steps)
        last = n_rows // blk_rows - 1
        ins.append(stack)
        in_specs.append(pl.BlockSpec((None, blk_rows, width),
                                     lambda s, idx=idx, last=last: (idx, jnp.minimum(s, last), 0)))
        outs.append(jax.ShapeDtypeStruct((n_rows, width), _BF16))
        out_specs.append(pl.BlockSpec((blk_rows, width),
                                      lambda s, last=last: (jnp.minimum(s, last), 0)))
    res = pl.pallas_call(
        _cast_kernel, grid=(n_steps,), in_specs=in_specs, out_specs=out_specs, out_shape=outs,
        compiler_params=pltpu.CompilerParams(dimension_semantics=("arbitrary",),
                                             vmem_limit_bytes=VMEM_LIMIT_BYTES),
        name="cast_layer0",
    )(*ins)
    return {name: arr for (name, _), arr in zip(_big_weights(0), res)}


def _to_groups(a, group):
    *lead, b, k, c = a.shape
    a = a.reshape(*lead, b // group, group, k, c)
    a = jnp.swapaxes(a, -3, -2)
    return a.reshape(*lead, b * k, c)


def _from_groups(a, group, k):
    *lead, n, c = a.shape
    a = a.reshape(*lead, n // (k * group), k, group, c)
    a = jnp.swapaxes(a, -3, -2)
    return a.reshape(*lead, n // k, k, c)


def kernel(x_prompt, x_sample, state_shortconv, state_pool, state_ffnconv, g_mix, g_ffn, g_final, a_w_in, a_conv, a_w_out, b_w_in, b_g_v, b_w_s, b_bias, b_w_out, c_w_group, c_scale, f_w_up, f_conv, f_conv_b, f_w_down):
    batch, seq, d = x_prompt.shape
    dec_batch, dec_seq, _ = x_sample.shape
    row = lambda a: a[:, None, :]
    w = dict(g_mix=row(g_mix), g_ffn=row(g_ffn), g_final=g_final[None, :],
             a_conv=a_conv, b_g_v=row(b_g_v), b_w_s=b_w_s, b_bias=b_bias,
             b_bias_t=jnp.swapaxes(b_bias, 1, 2), c_scale=row(c_scale),
             f_conv=f_conv, f_conv_b=row(f_conv_b))
    big = dict(a_w_in=a_w_in, a_w_out=a_w_out, b_w_in=b_w_in, b_w_out=b_w_out,
               c_w_group=c_w_group.reshape(c_w_group.shape[0], -1, c_w_group.shape[-1]),
               f_w_up=f_w_up, f_w_down=f_w_down)
    wbs = [_cast_layer0(big)]

    yp, conv_p, pool_p, ffn_p, _ = _trunk(
        x_prompt.reshape(batch * seq, d), None, None, None, w, wbs, big, prompt=True, group=batch)

    group = TILE_ROWS // dec_seq
    tg = functools.partial(_to_groups, group=group)
    ys, conv_s, pool_s, ffn_s, v_s = _trunk(
        tg(x_sample), tg(state_shortconv), tg(state_pool), tg(state_ffnconv), w, wbs, big,
        prompt=False, group=group)
    fg = functools.partial(_from_groups, group=group)

    return (yp.reshape(batch, seq, d),
            fg(ys, k=dec_seq),
            conv_p, fg(conv_s, k=CONV_W - 1),
            pool_p, fg(pool_s, k=POOL_STATE),
            ffn_p, fg(ffn_s, k=CONV_W - 1),
            fg(jnp.stack(v_s), k=dec_seq))
```
